```python
import math
import jax, jax.numpy as jnp
from jax import lax
import numpy as np

D_MODEL = 1024
BATCH = 4
SEQ = 8192
DEPTH = 4

N_MIXERS = 2
N_META = 16
GRID_W = 64
NA_KH = 8
NA_KW = 16
NA_HEADS = 16
NA_HEAD_DIM = D_MODEL // NA_HEADS
DA_HEADS = 8
DA_HEAD_DIM = D_MODEL // (2 * DA_HEADS)
T5_BUCKETS = 32
T5_MAX_DIST = 128
Q_BLOCK = 128
D_FF = int(math.ceil(8 * D_MODEL / 3 / 128)) * 128
FFN_RES = 0.5
RMS_EPS = 1e-6
N_A_LAYERS = (DEPTH + 1) // 2
N_B_LAYERS = DEPTH // 2

kernel_name = "hybrid_natten_diffattn_macaron_encoder"


def rmsnorm(x, g):
    xf = x.astype(jnp.float32)
    y = xf * lax.rsqrt(jnp.mean(xf * xf, axis=-1, keepdims=True) + RMS_EPS)
    return (y * g.astype(jnp.float32)).astype(x.dtype)


def swiglu(x, w_gate, w_up, w_down):
    return (jax.nn.silu(x @ w_gate) * (x @ w_up)) @ w_down


def t5_bucket(rel):
    nb = T5_BUCKETS // 2
    max_exact = nb // 2
    ret = jnp.where(rel > 0, nb, 0)
    n = jnp.abs(rel)
    nf = jnp.maximum(n, 1).astype(jnp.float32)
    large = max_exact + (jnp.log(nf / max_exact) / math.log(T5_MAX_DIST / max_exact)
                         * (nb - max_exact)).astype(jnp.int32)
    large = jnp.minimum(large, nb - 1)
    return ret + jnp.where(n < max_exact, n, large)


def neighborhood_attention(h, w_qkv, b_qkv, w_o, b_o, rpb, meta_bias):
    B, T, _ = h.shape
    n_tok = T - N_META
    rows = n_tok // GRID_W
    kh = min(NA_KH, rows)
    qkv = (h @ w_qkv + b_qkv).reshape(B, T, 3, NA_HEADS, NA_HEAD_DIM)
    q = qkv[:, :, 0] * (NA_HEAD_DIM ** -0.5)
    k = qkv[:, :, 1]
    v = qkv[:, :, 2]
    qm, km, vm = q[:, :N_META], k[:, :N_META], v[:, :N_META]
    grid_shape = (B, rows, GRID_W, NA_HEADS, NA_HEAD_DIM)
    qg = q[:, N_META:].reshape(grid_shape)
    kg = k[:, N_META:].reshape(grid_shape)
    vg = v[:, N_META:].reshape(grid_shape)

    lm = jnp.einsum('bqhd,bmhd->bhqm', qm, km).astype(jnp.float32) + meta_bias[None, :, None, :]
    om = jnp.einsum('bhqm,bmhd->bqhd', jax.nn.softmax(lm, axis=-1).astype(vm.dtype), vm)

    cols = np.arange(GRID_W)
    cs = np.clip(cols - NA_KW // 2, 0, GRID_W - NA_KW)
    col_idx = cs[:, None] + np.arange(NA_KW)[None, :]
    dx = col_idx - cols[:, None] + (NA_KW - 1)

    def row_fn(r):
        rs = jnp.clip(r - kh // 2, 0, rows - kh)
        k_rows = lax.dynamic_slice_in_dim(kg, rs, kh, axis=1)
        v_rows = lax.dynamic_slice_in_dim(vg, rs, kh, axis=1)
        k_win = k_rows[:, :, col_idx]
        v_win = v_rows[:, :, col_idx]
        q_row = lax.dynamic_index_in_dim(qg, r, axis=1, keepdims=False)
        dy = rs + jnp.arange(kh) - r + (NA_KH - 1)
        bias = rpb[:, dy][:, :, dx].transpose(0, 2, 1, 3)
        ln = jnp.einsum('bchd,brckhd->bhcrk', q_row, k_win).astype(jnp.float32) + bias[None]
        lmeta = jnp.einsum('bchd,bmhd->bhcm', q_row, km).astype(jnp.float32) + meta_bias[None, :, None, :]
        logits = jnp.concatenate([ln.reshape(B, NA_HEADS, GRID_W, kh * NA_KW), lmeta], axis=-1)
        p = jax.nn.softmax(logits, axis=-1).astype(v.dtype)
        pn = p[..., :kh * NA_KW].reshape(B, NA_HEADS, GRID_W, kh, NA_KW)
        pm = p[..., kh * NA_KW:]
        return (jnp.einsum('bhcrk,brckhd->bchd', pn, v_win)
                + jnp.einsum('bhcm,bmhd->bchd', pm, vm))

    og = lax.map(row_fn, jnp.arange(rows))
    og = og.transpose(1, 0, 2, 3, 4).reshape(B, n_tok, D_MODEL)
    o = jnp.concatenate([om.reshape(B, N_META, D_MODEL), og], axis=1)
    return o @ w_o + b_o


def diff_attention(h, w_qkv, w_o, lam_p, subln_g, rel_table, lambda_init):
    B, T, _ = h.shape
    n_tok = T - N_META
    nb = n_tok // Q_BLOCK
    d = DA_HEAD_DIM
    q, k, v = jnp.split(h @ w_qkv, 3, axis=-1)
    q = (q * (d ** -0.5)).reshape(B, T, DA_HEADS, 2, d).transpose(0, 2, 3, 1, 4)
    k = k.reshape(B, T, DA_HEADS, 2, d).transpose(0, 2, 3, 1, 4)
    v = v.reshape(B, T, DA_HEADS, 2 * d).transpose(0, 2, 1, 3)
    lp = lam_p.astype(jnp.float32)
    lam = jnp.exp(jnp.sum(lp[0] * lp[1])) - jnp.exp(jnp.sum(lp[2] * lp[3])) + lambda_init
    key_pos = jnp.arange(T)

    def attend(q_blk, q_pos):
        bias = rel_table[t5_bucket(key_pos[None, :] - q_pos[:, None])]
        logits = (jnp.einsum('bhsqd,bhskd->bhsqk', q_blk, k).astype(jnp.float32)
                  + bias.transpose(2, 0, 1).astype(jnp.float32)[None, :, None])
        p = jax.nn.softmax(logits, axis=-1)
        a = (p[:, :, 0] - lam * p[:, :, 1]).astype(v.dtype)
        return jnp.einsum('bhqk,bhkd->bhqd', a, v)

    om = attend(q[:, :, :, :N_META], jnp.arange(N_META))
    qr = q[:, :, :, N_META:].reshape(B, DA_HEADS, 2, nb, Q_BLOCK, d).transpose(3, 0, 1, 2, 4, 5)

    def block_fn(args):
        q_blk, i = args
        return attend(q_blk, N_META + i * Q_BLOCK + jnp.arange(Q_BLOCK))

    orr = lax.map(block_fn, (qr, jnp.arange(nb)))
    orr = orr.transpose(1, 2, 0, 3, 4).reshape(B, DA_HEADS, n_tok, 2 * d)
    o = jnp.concatenate([om, orr], axis=2)
    o = rmsnorm(o, subln_g) * (1.0 - lambda_init)
    o = o.transpose(0, 2, 1, 3).reshape(B, T, D_MODEL)
    return o @ w_o


def setup_inputs(seed: int = 0) -> dict:
    key = jax.random.key(seed)
    ks = jax.random.split(key, 20)
    D, F = D_MODEL, D_FF
    nrm = jax.random.normal
    return {
        "x": nrm(ks[0], (BATCH, SEQ, D), jnp.float32),
        "meta_tokens": nrm(ks[1], (N_META, D), jnp.float32),
        "norm_g": 1.0 + 0.02 * nrm(ks[2], (DEPTH, 6, D), jnp.float32),
        "ffn_w_gate": nrm(ks[3], (DEPTH, 2, D, F), jnp.float32) * D ** -0.5,
        "ffn_w_up": nrm(ks[4], (DEPTH, 2, D, F), jnp.float32) * D ** -0.5,
        "ffn_w_down": nrm(ks[5], (DEPTH, 2, F, D), jnp.float32) * F ** -0.5,
        "na_w_qkv": nrm(ks[6], (N_A_LAYERS, D, 3 * D), jnp.float32) * D ** -0.5,
        "na_b_qkv": 0.02 * nrm(ks[7], (N_A_LAYERS, 3 * D), jnp.float32),
        "na_w_o": nrm(ks[8], (N_A_LAYERS, D, D), jnp.float32) * D ** -0.5,
        "na_b_o": 0.02 * nrm(ks[9], (N_A_LAYERS, D), jnp.float32),
        "na_rpb": 0.1 * nrm(ks[10], (N_A_LAYERS, NA_HEADS, 2 * NA_KH - 1, 2 * NA_KW - 1), jnp.float32),
        "na_meta_bias": 0.1 * nrm(ks[11], (N_A_LAYERS, NA_HEADS, N_META), jnp.float32),
        "da_w_qkv": nrm(ks[12], (N_B_LAYERS, D, 3 * D), jnp.float32) * D ** -0.5,
        "da_w_o": nrm(ks[13], (N_B_LAYERS, D, D), jnp.float32) * D ** -0.5,
        "da_lambda": 0.1 * nrm(ks[14], (N_B_LAYERS, 4, DA_HEAD_DIM), jnp.float32),
        "da_subln_g": 1.0 + 0.02 * nrm(ks[15], (N_B_LAYERS, 2 * DA_HEAD_DIM), jnp.float32),
        "t5_rel_bias": 0.1 * nrm(ks[16], (T5_BUCKETS, DA_HEADS), jnp.float32),
    }


def reference(x, meta_tokens, norm_g, ffn_w_gate, ffn_w_up, ffn_w_down,
              na_w_qkv, na_b_qkv, na_w_o, na_b_o, na_rpb, na_meta_bias,
              da_w_qkv, da_w_o, da_lambda, da_subln_g, t5_rel_bias):
    B = x.shape[0]
    meta = jnp.broadcast_to(meta_tokens[None].astype(x.dtype), (B, N_META, D_MODEL))
    h = jnp.concatenate([meta, x], axis=1)
    for i in range(DEPTH):
        g = norm_g[i]
        f1 = swiglu(rmsnorm(h, g[0]), ffn_w_gate[i, 0], ffn_w_up[i, 0], ffn_w_down[i, 0])
        h = h + FFN_RES * rmsnorm(f1, g[1])
        j = i // N_MIXERS
        hn = rmsnorm(h, g[2])
        if i % N_MIXERS == 0:
            m = neighborhood_attention(hn, na_w_qkv[j], na_b_qkv[j], na_w_o[j], na_b_o[j],
                                       na_rpb[j], na_meta_bias[j])
        else:
            lambda_init = 0.8 - 0.6 * math.exp(-0.3 * i)
            m = diff_attention(hn, da_w_qkv[j], da_w_o[j], da_lambda[j], da_subln_g[j],
                               t5_rel_bias, lambda_init)
        h = h + rmsnorm(m, g[3])
        f2 = swiglu(rmsnorm(h, g[4]), ffn_w_gate[i, 1], ffn_w_up[i, 1], ffn_w_down[i, 1])
        h = h + FFN_RES * rmsnorm(f2, g[5])
    return h[:, N_META:]
```

```python
import functools
import math

import numpy as np
import jax
import jax.numpy as jnp
from jax import lax
from jax.experimental import pallas as pl
from jax.experimental.pallas import tpu as pltpu

F32 = jnp.float32
BF16 = jnp.bfloat16

D_MODEL = 1024
N_META = 16
GRID_W = 64
NA_KH = 8
NA_KW = 16
NA_HEADS = 16
DA_HEADS = 8
DA_HEAD_DIM = 64
T5_BUCKETS = 32
T5_MAX_DIST = 128
D_FF = 2816
FFN_RES = 0.5
RMS_EPS = 1e-6
N_MIXERS = 2

LANES = 128
META_BLK = 128
ROW_TILE = 512
ATT_BLK = 512
NEG = -1e30
VMEM_LIMIT = 56 * 1024 * 1024


def _rms(x, g):
    return x * lax.rsqrt(jnp.mean(x * x, axis=-1, keepdims=True) + RMS_EPS) * g


def _params(sem):
    return pltpu.CompilerParams(dimension_semantics=sem, vmem_limit_bytes=VMEM_LIMIT)


def _resident(shape):
    return pl.BlockSpec(shape, lambda *_: (0,) * len(shape), pipeline_mode=pl.Buffered(1))


def _ffn_kernel(h_ref, gin_ref, gout_ref, wg_ref, wu_ref, wd_ref, o_ref, *, fc):
    x = h_ref[...]
    xn = _rms(x, gin_ref[...]).astype(BF16)
    acc = None
    for c in range(D_FF // fc):
        sl = slice(c * fc, (c + 1) * fc)
        g = jnp.dot(xn, wg_ref[:, sl], preferred_element_type=F32)
        u = jnp.dot(xn, wu_ref[:, sl], preferred_element_type=F32)
        hid = (g * jax.nn.sigmoid(g) * u).astype(BF16)
        d = jnp.dot(hid, wd_ref[sl, :], preferred_element_type=F32)
        acc = d if acc is None else acc + d
    o_ref[...] = x + FFN_RES * _rms(acc, gout_ref[...])


def _ffn(h, g_in, g_out, wg, wu, wd):
    mt = h.shape[0]
    row = pl.BlockSpec((ROW_TILE, D_MODEL), lambda i: (i, 0))
    return pl.pallas_call(
        functools.partial(_ffn_kernel, fc=1408),
        out_shape=jax.ShapeDtypeStruct((mt, D_MODEL), F32),
        grid=(mt // ROW_TILE,),
        in_specs=[row, _resident((1, D_MODEL)), _resident((1, D_MODEL)),
                  _resident((D_MODEL, D_FF)), _resident((D_MODEL, D_FF)), _resident((D_FF, D_MODEL))],
        out_specs=row,
        compiler_params=_params(("parallel",)),
        name="ffn",
    )(h, g_in, g_out, wg, wu, wd)


def _qkv_kernel(h_ref, g_ref, w_ref, b_ref, o_ref, *, q_scale):
    xn = _rms(h_ref[...], g_ref[...]).astype(BF16)
    for j in range(3):
        sl = slice(j * D_MODEL, (j + 1) * D_MODEL)
        y = jnp.dot(xn, w_ref[:, sl], preferred_element_type=F32) + b_ref[:, sl]
        if j == 0:
            y = y * q_scale
        o_ref[:, sl] = y.astype(BF16)


def _qkv(h, g, w, b, q_scale):
    mt = h.shape[0]
    return pl.pallas_call(
        functools.partial(_qkv_kernel, q_scale=q_scale),
        out_shape=jax.ShapeDtypeStruct((mt, 3 * D_MODEL), BF16),
        grid=(mt // ROW_TILE,),
        in_specs=[pl.BlockSpec((ROW_TILE, D_MODEL), lambda i: (i, 0)), _resident((1, D_MODEL)),
                  _resident((D_MODEL, 3 * D_MODEL)), _resident((1, 3 * D_MODEL))],
        out_specs=pl.BlockSpec((ROW_TILE, 3 * D_MODEL), lambda i: (i, 0)),
        compiler_params=_params(("parallel",)),
        name="qkv_proj",
    )(h, g, w, b)


def _oproj_kernel(a_ref, h_ref, w_ref, b_ref, g_ref, o_ref):
    m = jnp.dot(a_ref[...], w_ref[...], preferred_element_type=F32) + b_ref[...]
    o_ref[...] = h_ref[...] + _rms(m, g_ref[...])


def _oproj(a, h, w, b, g):
    mt = h.shape[0]
    row = pl.BlockSpec((ROW_TILE, D_MODEL), lambda i: (i, 0))
    return pl.pallas_call(
        _oproj_kernel,
        out_shape=jax.ShapeDtypeStruct((mt, D_MODEL), F32),
        grid=(mt // ROW_TILE,),
        in_specs=[row, row, _resident((D_MODEL, D_MODEL)), _resident((1, D_MODEL)), _resident((1, D_MODEL))],
        out_specs=row,
        compiler_params=_params(("parallel",)),
        name="out_proj",
    )(a, h, w, b, g)


def _na_kernel(q_ref, k_ref, v_ref, nab_ref, mb_ref, o_ref, *, rows, seq):
    kh = NA_KH
    lane = lax.broadcasted_iota(jnp.int32, (GRID_W, LANES), 1)
    head0 = lane < (LANES // 2)
    k_meta = k_ref[0, seq:seq + META_BLK, :]
    v_meta = v_ref[0, seq:seq + META_BLK, :]
    mb = jnp.concatenate([jnp.broadcast_to(mb_ref[0, 0], (GRID_W, LANES)),
                          jnp.broadcast_to(mb_ref[0, 1], (GRID_W, LANES))], axis=0)
    nt = (((1,), (1,)), ((), ()))

    def split_heads(q):
        zero = jnp.zeros_like(q)
        return jnp.concatenate([jnp.where(head0[:q.shape[0]], q, zero),
                                jnp.where(head0[:q.shape[0]], zero, q)], axis=0)

    def row_fn(r, carry):
        rs = jnp.clip(r - kh // 2, 0, rows - kh)
        dy0 = rs - r + (NA_KH - 1)
        q2 = split_heads(q_ref[0, pl.ds(pl.multiple_of(r * GRID_W, GRID_W), GRID_W), :])
        kbase = pl.multiple_of(rs * GRID_W, GRID_W)
        k_win = k_ref[0, pl.ds(kbase, kh * GRID_W), :]
        v_win = v_ref[0, pl.ds(kbase, kh * GRID_W), :]
        bias = jnp.concatenate([nab_ref[0, 0, dy0], nab_ref[0, 1, dy0]], axis=0)
        s = lax.dot_general(q2, k_win, nt, preferred_element_type=F32) + bias
        sm = lax.dot_general(q2, k_meta, nt, preferred_element_type=F32) + mb
        m = jnp.maximum(jnp.max(s, axis=-1, keepdims=True), jnp.max(sm, axis=-1, keepdims=True))
        p = jnp.exp(s - m)
        pm = jnp.exp(sm - m)
        l = jnp.sum(p, axis=-1, keepdims=True) + jnp.sum(pm, axis=-1, keepdims=True)
        o2 = (jnp.dot(p.astype(BF16), v_win, preferred_element_type=F32)
              + jnp.dot(pm.astype(BF16), v_meta, preferred_element_type=F32)) / l
        o = jnp.where(head0, o2[:GRID_W], o2[GRID_W:])
        o_ref[0, pl.ds(pl.multiple_of(r * GRID_W, GRID_W), GRID_W), :] = o.astype(BF16)
        return carry

    lax.fori_loop(0, rows, row_fn, 0)

    qm = q_ref[0, seq:seq + META_BLK, :]
    lane_m = lax.broadcasted_iota(jnp.int32, (META_BLK, LANES), 1) < (LANES // 2)
    zero = jnp.zeros_like(qm)
    q2 = jnp.concatenate([jnp.where(lane_m, qm, zero), jnp.where(lane_m, zero, qm)], axis=0)
    mbm = jnp.concatenate([jnp.broadcast_to(mb_ref[0, 0], (META_BLK, LANES)),
                           jnp.broadcast_to(mb_ref[0, 1], (META_BLK, LANES))], axis=0)
    sm = lax.dot_general(q2, k_meta, nt, preferred_element_type=F32) + mbm
    pm = jnp.exp(sm - jnp.max(sm, axis=-1, keepdims=True))
    o2 = jnp.dot(pm.astype(BF16), v_meta, preferred_element_type=F32) / jnp.sum(pm, axis=-1, keepdims=True)
    o_ref[0, seq:seq + META_BLK, :] = jnp.where(lane_m, o2[:META_BLK], o2[META_BLK:]).astype(BF16)


def _na_tables(rpb, meta_bias, rows):
    kh = NA_KH
    qc = np.arange(GRID_W)[:, None]
    kc = np.arange(GRID_W)[None, :]
    cs = np.clip(qc - NA_KW // 2, 0, GRID_W - NA_KW)
    valid = (kc >= cs) & (kc < cs + NA_KW)
    dx = np.clip(kc - qc + NA_KW - 1, 0, 2 * NA_KW - 2)
    dy = np.arange(kh)[:, None] + np.arange(kh)[None, :]
    t = rpb[:, dy][:, :, :, dx]
    t = jnp.where(valid[None, None, None], t, NEG)
    t = t.transpose(0, 1, 3, 2, 4).reshape(NA_HEADS // 2, 2, kh, GRID_W, kh * GRID_W)
    mb = jnp.full((NA_HEADS, LANES), NEG, F32).at[:, :N_META].set(meta_bias)
    return t.astype(F32), mb.reshape(NA_HEADS // 2, 2, 1, LANES)


def _na_attention(qkv, nab, mb, seq):
    b, tp, _ = qkv.shape
    rows = seq // GRID_W
    n_slab = D_MODEL // LANES
    slab = lambda off: pl.BlockSpec((1, tp, LANES), lambda i, s: (i, 0, off + s))
    return pl.pallas_call(
        functools.partial(_na_kernel, rows=rows, seq=seq),
        out_shape=jax.ShapeDtypeStruct((b, tp, D_MODEL), BF16),
        grid=(b, n_slab),
        in_specs=[slab(0), slab(n_slab), slab(2 * n_slab),
                  pl.BlockSpec((1, 2, NA_KH, GRID_W, NA_KH * GRID_W), lambda i, s: (s, 0, 0, 0, 0)),
                  pl.BlockSpec((1, 2, 1, LANES), lambda i, s: (s, 0, 0, 0))],
        out_specs=slab(0),
        compiler_params=_params(("parallel", "parallel")),
        name="na_attention",
    )(qkv, qkv, qkv, nab, mb)


def _da_kernel(cst_ref, q_ref, k_ref, v_ref, bt_ref, btm_ref, lamp_ref, g_ref, o_ref,
               vt_scr, vtm_scr, q2_scr, m_scr, l_scr, acc_scr, *, bq, nkb, seq, meta_queries, lambda_init):
    bk = ATT_BLK
    hd = pl.program_id(1)
    qi = pl.program_id(2)
    nt = (((1,), (1,)), ((), ()))

    @pl.when(qi == 0)
    def _():
        for kb in range(nkb):
            vt_scr[kb] = v_ref[0, kb * bk:(kb + 1) * bk, :].astype(F32).T.astype(BF16)
        vtm_scr[...] = v_ref[0, seq:seq + META_BLK, :].astype(F32).T.astype(BF16)

    q = q_ref[0]
    first = lax.broadcasted_iota(jnp.int32, q.shape, 1) < DA_HEAD_DIM
    zero = jnp.zeros_like(q)
    q2_scr[0:bq, :] = jnp.where(first, q, zero)
    q2_scr[bq:2 * bq, :] = jnp.where(first, zero, q)
    m_scr[...] = jnp.full(m_scr.shape, NEG, F32)
    l_scr[...] = jnp.zeros(l_scr.shape, F32)
    acc_scr[...] = jnp.zeros(acc_scr.shape, F32)
    c_neg = cst_ref[hd, 0]
    c_pos = cst_ref[hd, 1]

    def logits(k_blk):
        return lax.dot_general(k_blk, q2_scr[...], nt, preferred_element_type=F32)

    def update(s, vt, c):
        m_old = m_scr[...]
        m_new = jnp.maximum(m_old, jnp.max(s, axis=0, keepdims=True) + c)
        p = jnp.exp(s - (m_new - c))
        alpha = jnp.exp(m_old - m_new)
        l_scr[...] = alpha * l_scr[...] + jnp.sum(p, axis=0, keepdims=True)
        acc_scr[...] = acc_scr[...] * alpha + jnp.dot(vt, p.astype(BF16), preferred_element_type=F32)
        m_scr[...] = m_new

    if meta_queries:
        lo, hi = 0, 1
    else:
        lo = jnp.maximum(qi - 1, 0)
        hi = jnp.minimum(qi + 2, nkb)

    def far_body(j, carry):
        kb = jnp.where(j < lo, j, j + (hi - lo))
        c = jnp.where(j < lo, c_neg, c_pos)
        k_blk = k_ref[0, pl.ds(pl.multiple_of(kb * bk, bk), bk), :]
        update(logits(k_blk), vt_scr[kb], c)
        return carry

    lax.fori_loop(0, nkb - (hi - lo), far_body, 0)

    def near_body(kb, carry):
        t = 0 if meta_queries else kb - qi + 1
        bias = bt_ref[0, t]
        k_blk = k_ref[0, pl.ds(pl.multiple_of(kb * bk, bk), bk), :]
        update(logits(k_blk) + jnp.concatenate([bias, bias], axis=1), vt_scr[kb], 0.0)
        return carry

    lax.fori_loop(lo, hi, near_body, 0)

    tm = 0 if meta_queries else jnp.minimum(qi, 1)
    bias = btm_ref[0, tm]
    update(logits(k_ref[0, seq:seq + META_BLK, :]) + jnp.concatenate([bias, bias], axis=1), vtm_scr[...], 0.0)

    lp = lamp_ref[...]
    lam = (jnp.exp(jnp.sum(lp[0:1] * lp[1:2], axis=1, keepdims=True))
           - jnp.exp(jnp.sum(lp[2:3] * lp[3:4], axis=1, keepdims=True)) + lambda_init)
    on = acc_scr[...] / l_scr[...]
    ot = on[:, :bq] - lam * on[:, bq:]
    y = ot * lax.rsqrt(jnp.mean(ot * ot, axis=0, keepdims=True) + RMS_EPS) * g_ref[...]
    o_ref[0] = (y * (1.0 - lambda_init)).T.astype(BF16)


def _t5_bucket(rel):
    nb = T5_BUCKETS // 2
    max_exact = nb // 2
    ret = jnp.where(rel > 0, nb, 0)
    n = jnp.abs(rel)
    nf = jnp.maximum(n, 1).astype(jnp.float32)
    large = max_exact + (jnp.log(nf / max_exact) / math.log(T5_MAX_DIST / max_exact)
                         * (nb - max_exact)).astype(jnp.int32)
    large = jnp.minimum(large, nb - 1)
    return ret + jnp.where(n < max_exact, n, large)


def _da_tables(rel_table, seq):
    blk = ATT_BLK
    lookup = lambda rel: rel_table[_t5_bucket(jnp.asarray(rel, jnp.int32))].transpose(2, 0, 1).astype(F32)
    kk = np.arange(blk)[:, None]
    qq = np.arange(blk)[None, :]
    bt = jnp.stack([lookup(d * blk + kk - qq) for d in (-1, 0, 1)], axis=1)
    jm = np.arange(META_BLK)[:, None]
    pad_key = (jm >= N_META)
    first = lookup(jm - (qq + N_META))
    later = lookup(jm - (qq + N_META + blk))
    btm = jnp.where(pad_key[None, None], NEG, jnp.stack([first, later], axis=1))
    im = np.minimum(np.arange(META_BLK), N_META - 1)[None, :]
    bt_q = lookup(kk + N_META - im)[:, None]
    btm_q = jnp.where(pad_key[None, None], NEG, lookup(jm - im)[:, None])
    cst = jnp.stack([rel_table[T5_BUCKETS // 2 - 1], rel_table[T5_BUCKETS - 1]], axis=1).astype(F32)
    return cst, bt, btm, bt_q, btm_q


def _da_call(qkv, prev_out, cst, bt, btm, lam_p, g_col, *, seq, meta_queries, lambda_init):
    b, tp, _ = qkv.shape
    nkb = seq // ATT_BLK
    bq = META_BLK if meta_queries else ATT_BLK
    nq = 1 if meta_queries else nkb
    q_row0 = seq // bq if meta_queries else 0
    n_t, n_tm = bt.shape[1], btm.shape[1]
    in_specs = [
        pl.BlockSpec(memory_space=pltpu.SMEM),
        pl.BlockSpec((1, bq, LANES), lambda i, h, j: (i, q_row0 + j, h)),
        pl.BlockSpec((1, tp, LANES), lambda i, h, j: (i, 0, DA_HEADS + h)),
        pl.BlockSpec((1, tp, LANES), lambda i, h, j: (i, 0, 2 * DA_HEADS + h)),
        pl.BlockSpec((1, n_t, ATT_BLK, bq), lambda i, h, j: (h, 0, 0, 0)),
        pl.BlockSpec((1, n_tm, META_BLK, bq), lambda i, h, j: (h, 0, 0, 0)),
        pl.BlockSpec((4, DA_HEAD_DIM), lambda i, h, j: (0, 0)),
        pl.BlockSpec((LANES, 1), lambda i, h, j: (0, 0)),
    ]
    args = [cst, qkv, qkv, qkv, bt, btm, lam_p, g_col]
    aliases = {}
    if prev_out is not None:
        in_specs.append(pl.BlockSpec(memory_space=pl.ANY))
        args.append(prev_out)
        aliases = {len(args) - 1: 0}
    kern = functools.partial(_da_kernel, bq=bq, nkb=nkb, seq=seq, meta_queries=meta_queries,
                             lambda_init=lambda_init)
    if prev_out is not None:
        body = kern
        kern = lambda *refs: body(*refs[:8], *refs[9:])
    return pl.pallas_call(
        kern,
        out_shape=jax.ShapeDtypeStruct((b, tp, D_MODEL), BF16),
        grid=(b, DA_HEADS, nq),
        in_specs=in_specs,
        out_specs=pl.BlockSpec((1, bq, LANES), lambda i, h, j: (i, q_row0 + j, h)),
        scratch_shapes=[pltpu.VMEM((nkb, LANES, ATT_BLK), BF16), pltpu.VMEM((LANES, META_BLK), BF16),
                        pltpu.VMEM((2 * bq, LANES), BF16), pltpu.VMEM((1, 2 * bq), F32),
                        pltpu.VMEM((1, 2 * bq), F32), pltpu.VMEM((LANES, 2 * bq), F32)],
        input_output_aliases=aliases,
        compiler_params=_params(("parallel", "parallel", "arbitrary")),
        name="da_meta_queries" if meta_queries else "da_attention",
    )(*args)


def kernel(x, meta_tokens, norm_g, ffn_w_gate, ffn_w_up, ffn_w_down, na_w_qkv, na_b_qkv, na_w_o, na_b_o, na_rpb, na_meta_bias, da_w_qkv, da_w_o, da_lambda, da_subln_g, t5_rel_bias):
    b, seq, d = x.shape
    depth = norm_g.shape[0]
    assert d == D_MODEL and seq % ATT_BLK == 0 and seq % GRID_W == 0
    tp = seq + META_BLK
    assert (b * tp) % ROW_TILE == 0
    meta = jnp.broadcast_to(meta_tokens[None].astype(x.dtype), (b, N_META, d))
    pad = jnp.zeros((b, META_BLK - N_META, d), x.dtype)
    h = jnp.concatenate([x, meta, pad], axis=1).reshape(b * tp, d)

    vec = lambda v: v.reshape(1, -1).astype(F32)
    zero_qkv_bias = jnp.zeros((1, 3 * d), F32)
    zero_o_bias = jnp.zeros((1, d), F32)
    da_cst = _da_tables(t5_rel_bias, seq)

    for i in range(depth):
        g = norm_g[i]
        j = i // N_MIXERS
        h = _ffn(h, vec(g[0]), vec(g[1]), ffn_w_gate[i, 0].astype(BF16), ffn_w_up[i, 0].astype(BF16),
                 ffn_w_down[i, 0].astype(BF16))
        if i % N_MIXERS == 0:
            qkv = _qkv(h, vec(g[2]), na_w_qkv[j].astype(BF16), vec(na_b_qkv[j]), (D_MODEL // NA_HEADS) ** -0.5)
            nab, mb = _na_tables(na_rpb[j], na_meta_bias[j], seq // GRID_W)
            a = _na_attention(qkv.reshape(b, tp, 3 * d), nab, mb, seq)
            h = _oproj(a.reshape(b * tp, d), h, na_w_o[j].astype(BF16), vec(na_b_o[j]), vec(g[3]))
        else:
            lambda_init = 0.8 - 0.6 * math.exp(-0.3 * i)
            qkv = _qkv(h, vec(g[2]), da_w_qkv[j].astype(BF16), zero_qkv_bias, DA_HEAD_DIM ** -0.5)
            qkv = qkv.reshape(b, tp, 3 * d)
            cst, bt, btm, bt_q, btm_q = da_cst
            g_col = da_subln_g[j].reshape(LANES, 1).astype(F32)
            lam_p = da_lambda[j].astype(F32)
            a = _da_call(qkv, None, cst, bt, btm, lam_p, g_col, seq=seq, meta_queries=False,
                         lambda_init=lambda_init)
            a = _da_call(qkv, a, cst, bt_q, btm_q, lam_p, g_col, seq=seq, meta_queries=True,
                         lambda_init=lambda_init)
            h = _oproj(a.reshape(b * tp, d), h, da_w_o[j].astype(BF16), zero_o_bias, vec(g[3]))
        h = _ffn(h, vec(g[4]), vec(g[5]), ffn_w_gate[i, 1].astype(BF16), ffn_w_up[i, 1].astype(BF16),
                 ffn_w_down[i, 1].astype(BF16))
    return h.reshape(b, tp, d)[:, :seq]
```

```python
import functools
import math

import numpy as np
import jax
import jax.numpy as jnp
from jax import lax
from jax.experimental import pallas as pl
from jax.experimental.pallas import tpu as pltpu

F32 = jnp.float32
BF16 = jnp.bfloat16

D_MODEL = 1024
N_META = 16
GRID_W = 64
NA_KH = 8
NA_KW = 16
NA_HEADS = 16
DA_HEADS = 8
DA_HEAD_DIM = 64
T5_BUCKETS = 32
T5_MAX_DIST = 128
D_FF = 2816
FFN_RES = 0.5
RMS_EPS = 1e-6
N_MIXERS = 2

LANES = 128
META_BLK = 128
ROW_TILE = 512
ATT_BLK = 512
COL_CHUNK = 256
ROW_PIECE = 64
LOGIT_BUFS = 3
NEG = -1e30
VMEM_LIMIT = 56 * 1024 * 1024


def _rms(x, g):
    return x * lax.rsqrt(jnp.mean(x * x, axis=-1, keepdims=True) + RMS_EPS) * g


def _params(sem):
    return pltpu.CompilerParams(dimension_semantics=sem, vmem_limit_bytes=VMEM_LIMIT)


def _resident(shape):
    return pl.BlockSpec(shape, lambda *_: (0,) * len(shape), pipeline_mode=pl.Buffered(1))


def _ffn_kernel(h_ref, gin_ref, gout_ref, wg_ref, wu_ref, wd_ref, o_ref, *, fc):
    x = h_ref[...]
    xn = _rms(x, gin_ref[...]).astype(BF16)
    acc = None
    for c in range(D_FF // fc):
        sl = slice(c * fc, (c + 1) * fc)
        g = jnp.dot(xn, wg_ref[:, sl], preferred_element_type=F32)
        u = jnp.dot(xn, wu_ref[:, sl], preferred_element_type=F32)
        hid = (g * jax.nn.sigmoid(g) * u).astype(BF16)
        d = jnp.dot(hid, wd_ref[sl, :], preferred_element_type=F32)
        acc = d if acc is None else acc + d
    o_ref[...] = x + FFN_RES * _rms(acc, gout_ref[...])


def _ffn(h, g_in, g_out, wg, wu, wd):
    mt = h.shape[0]
    row = pl.BlockSpec((ROW_TILE, D_MODEL), lambda i: (i, 0))
    return pl.pallas_call(
        functools.partial(_ffn_kernel, fc=1408),
        out_shape=jax.ShapeDtypeStruct((mt, D_MODEL), F32),
        grid=(mt // ROW_TILE,),
        in_specs=[row, _resident((1, D_MODEL)), _resident((1, D_MODEL)),
                  _resident((D_MODEL, D_FF)), _resident((D_MODEL, D_FF)), _resident((D_FF, D_MODEL))],
        out_specs=row,
        compiler_params=_params(("parallel",)),
        name="ffn",
    )(h, g_in, g_out, wg, wu, wd)


def _qkv_kernel(h_ref, g_ref, w_ref, b_ref, o_ref, *, q_scale):
    xn = _rms(h_ref[...], g_ref[...]).astype(BF16)
    for j in range(3):
        sl = slice(j * D_MODEL, (j + 1) * D_MODEL)
        y = jnp.dot(xn, w_ref[:, sl], preferred_element_type=F32) + b_ref[:, sl]
        if j == 0:
            y = y * q_scale
        o_ref[:, sl] = y.astype(BF16)


def _qkv(h, g, w, b, q_scale):
    mt = h.shape[0]
    return pl.pallas_call(
        functools.partial(_qkv_kernel, q_scale=q_scale),
        out_shape=jax.ShapeDtypeStruct((mt, 3 * D_MODEL), BF16),
        grid=(mt // ROW_TILE,),
        in_specs=[pl.BlockSpec((ROW_TILE, D_MODEL), lambda i: (i, 0)), _resident((1, D_MODEL)),
                  _resident((D_MODEL, 3 * D_MODEL)), _resident((1, 3 * D_MODEL))],
        out_specs=pl.BlockSpec((ROW_TILE, 3 * D_MODEL), lambda i: (i, 0)),
        compiler_params=_params(("parallel",)),
        name="qkv_proj",
    )(h, g, w, b)


def _oproj_kernel(a_ref, h_ref, w_ref, b_ref, g_ref, o_ref):
    m = jnp.dot(a_ref[...], w_ref[...], preferred_element_type=F32) + b_ref[...]
    o_ref[...] = h_ref[...] + _rms(m, g_ref[...])


def _oproj(a, h, w, b, g):
    mt = h.shape[0]
    row = pl.BlockSpec((ROW_TILE, D_MODEL), lambda i: (i, 0))
    return pl.pallas_call(
        _oproj_kernel,
        out_shape=jax.ShapeDtypeStruct((mt, D_MODEL), F32),
        grid=(mt // ROW_TILE,),
        in_specs=[row, row, _resident((D_MODEL, D_MODEL)), _resident((1, D_MODEL)), _resident((1, D_MODEL))],
        out_specs=row,
        compiler_params=_params(("parallel",)),
        name="out_proj",
    )(a, h, w, b, g)


def _na_kernel(q_ref, k_ref, v_ref, nab_ref, mb_ref, o_ref, *, rows, seq):
    kh = NA_KH
    lane = lax.broadcasted_iota(jnp.int32, (GRID_W, LANES), 1)
    head0 = lane < (LANES // 2)
    k_meta = k_ref[0, seq:seq + META_BLK, :]
    v_meta = v_ref[0, seq:seq + META_BLK, :]
    mb = jnp.concatenate([jnp.broadcast_to(mb_ref[0, 0], (GRID_W, LANES)),
                          jnp.broadcast_to(mb_ref[0, 1], (GRID_W, LANES))], axis=0)
    nt = (((1,), (1,)), ((), ()))

    def split_heads(q):
        zero = jnp.zeros_like(q)
        return jnp.concatenate([jnp.where(head0[:q.shape[0]], q, zero),
                                jnp.where(head0[:q.shape[0]], zero, q)], axis=0)

    def row_fn(r, carry):
        rs = jnp.clip(r - kh // 2, 0, rows - kh)
        dy0 = rs - r + (NA_KH - 1)
        q2 = split_heads(q_ref[0, pl.ds(pl.multiple_of(r * GRID_W, GRID_W), GRID_W), :])
        kbase = pl.multiple_of(rs * GRID_W, GRID_W)
        k_win = k_ref[0, pl.ds(kbase, kh * GRID_W), :]
        v_win = v_ref[0, pl.ds(kbase, kh * GRID_W), :]
        bias = jnp.concatenate([nab_ref[0, 0, dy0], nab_ref[0, 1, dy0]], axis=0)
        s = lax.dot_general(q2, k_win, nt, preferred_element_type=F32) + bias
        sm = lax.dot_general(q2, k_meta, nt, preferred_element_type=F32) + mb
        m = jnp.maximum(jnp.max(s, axis=-1, keepdims=True), jnp.max(sm, axis=-1, keepdims=True))
        p = jnp.exp(s - m)
        pm = jnp.exp(sm - m)
        l = jnp.sum(p, axis=-1, keepdims=True) + jnp.sum(pm, axis=-1, keepdims=True)
        o2 = (jnp.dot(p.astype(BF16), v_win, preferred_element_type=F32)
              + jnp.dot(pm.astype(BF16), v_meta, preferred_element_type=F32)) / l
        o = jnp.where(head0, o2[:GRID_W], o2[GRID_W:])
        o_ref[0, pl.ds(pl.multiple_of(r * GRID_W, GRID_W), GRID_W), :] = o.astype(BF16)
        return carry

    lax.fori_loop(0, rows, row_fn, 0)

    qm = q_ref[0, seq:seq + META_BLK, :]
    lane_m = lax.broadcasted_iota(jnp.int32, (META_BLK, LANES), 1) < (LANES // 2)
    zero = jnp.zeros_like(qm)
    q2 = jnp.concatenate([jnp.where(lane_m, qm, zero), jnp.where(lane_m, zero, qm)], axis=0)
    mbm = jnp.concatenate([jnp.broadcast_to(mb_ref[0, 0], (META_BLK, LANES)),
                           jnp.broadcast_to(mb_ref[0, 1], (META_BLK, LANES))], axis=0)
    sm = lax.dot_general(q2, k_meta, nt, preferred_element_type=F32) + mbm
    pm = jnp.exp(sm - jnp.max(sm, axis=-1, keepdims=True))
    o2 = jnp.dot(pm.astype(BF16), v_meta, preferred_element_type=F32) / jnp.sum(pm, axis=-1, keepdims=True)
    o_ref[0, seq:seq + META_BLK, :] = jnp.where(lane_m, o2[:META_BLK], o2[META_BLK:]).astype(BF16)


def _toeplitz(w, n, m):
    period = w.shape[-1]
    width = period - 1
    assert m <= width
    reps = -(-(n * width) // period)
    flat = jnp.tile(w, (1,) * (w.ndim - 1) + (reps,))[..., :n * width]
    return flat.reshape(w.shape[:-1] + (n, width))[..., :m]


def _na_tables(rpb, meta_bias, rows):
    kh = NA_KH
    qc = np.arange(GRID_W)[:, None]
    kc = np.arange(GRID_W)[None, :]
    cs = np.clip(qc - NA_KW // 2, 0, GRID_W - NA_KW)
    valid = (kc >= cs) & (kc < cs + NA_KW)
    period = 2 * GRID_W
    off = np.arange(period)
    off = np.where(off < GRID_W, off, off - period)
    w = rpb[:, :, np.clip(off + NA_KW - 1, 0, 2 * NA_KW - 2)]
    t = jnp.where(valid, _toeplitz(w, GRID_W, GRID_W), NEG)
    t = jnp.stack([t[:, dy0:dy0 + kh].transpose(0, 2, 1, 3) for dy0 in range(kh)], axis=1)
    t = t.reshape(NA_HEADS // 2, 2, kh, GRID_W, kh * GRID_W)
    mb = jnp.full((NA_HEADS, LANES), NEG, F32).at[:, :N_META].set(meta_bias)
    return t.astype(F32), mb.reshape(NA_HEADS // 2, 2, 1, LANES)


def _na_attention(qkv, nab, mb, seq):
    b, tp, _ = qkv.shape
    rows = seq // GRID_W
    n_slab = D_MODEL // LANES
    slab = lambda off: pl.BlockSpec((1, tp, LANES), lambda i, s: (i, 0, off + s))
    return pl.pallas_call(
        functools.partial(_na_kernel, rows=rows, seq=seq),
        out_shape=jax.ShapeDtypeStruct((b, tp, D_MODEL), BF16),
        grid=(b, n_slab),
        in_specs=[slab(0), slab(n_slab), slab(2 * n_slab),
                  pl.BlockSpec((1, 2, NA_KH, GRID_W, NA_KH * GRID_W), lambda i, s: (s, 0, 0, 0, 0)),
                  pl.BlockSpec((1, 2, 1, LANES), lambda i, s: (s, 0, 0, 0))],
        out_specs=slab(0),
        compiler_params=_params(("parallel", "parallel")),
        name="na_attention",
    )(qkv, qkv, qkv, nab, mb)


def _da_kernel(cst_ref, q_ref, k_ref, v_ref, bt_ref, btm_ref, lamp_ref, g_ref, o_ref,
               vt_scr, vtm_scr, q2_scr, *chunk_scr, bq, nkb, n_near, seq, meta_queries, lambda_init):
    bk = ATT_BLK
    n_chunk = 2 * bq // COL_CHUNK
    groups = [chunk_scr[i * n_chunk:(i + 1) * n_chunk] for i in range(4 + 2 * LOGIT_BUFS)]
    m_scr, l_scr, acc_scr, p_scr = groups[:4]
    s_scr, smax_scr = groups[4:4 + LOGIT_BUFS], groups[4 + LOGIT_BUFS:]
    hd = pl.program_id(1)
    qi = pl.program_id(2)
    nt = (((1,), (1,)), ((), ()))

    @pl.when(qi == 0)
    def _():
        for kb in range(nkb):
            vt_scr[kb] = v_ref[0, kb * bk:(kb + 1) * bk, :].astype(F32).T.astype(BF16)
        vtm_scr[...] = v_ref[0, seq:seq + META_BLK, :].astype(F32).T.astype(BF16)

    q = q_ref[0]
    first = lax.broadcasted_iota(jnp.int32, q.shape, 1) < DA_HEAD_DIM
    zero = jnp.zeros_like(q)
    q2_scr[0:bq, :] = jnp.where(first, q, zero)
    q2_scr[bq:2 * bq, :] = jnp.where(first, zero, q)
    for cc in range(n_chunk):
        m_scr[cc][...] = jnp.full(m_scr[cc].shape, NEG, F32)
        l_scr[cc][...] = jnp.zeros(l_scr[cc].shape, F32)
        acc_scr[cc][...] = jnp.zeros(acc_scr[cc].shape, F32)
    c_neg = cst_ref[hd, 0]
    c_pos = cst_ref[hd, 1]

    def logits(k_blk, buf, bias=None):
        nk = k_blk.shape[0]
        for cc in range(n_chunk):
            s = lax.dot_general(k_blk, q2_scr[cc * COL_CHUNK:(cc + 1) * COL_CHUNK, :], nt,
                                preferred_element_type=F32)
            if bias is not None:
                if bq >= COL_CHUNK:
                    b0 = (cc * COL_CHUNK) % bq
                    s = s + bias[:, b0:b0 + COL_CHUNK]
                else:
                    s = s + jnp.concatenate([bias[...]] * (COL_CHUNK // bq), axis=1)
            s_scr[buf][cc][0:nk, :] = s
            smax_scr[buf][cc][...] = jnp.max(s, axis=0, keepdims=True)

    def softmax_pv(vt, c, buf):
        nk = vt.shape[1]
        for cc in range(n_chunk):
            m_old = m_scr[cc][...]
            m_new = jnp.maximum(m_old, smax_scr[buf][cc][...] + c)
            shift = m_new - c
            psum = None
            for r in range(nk // ROW_PIECE):
                rws = slice(r * ROW_PIECE, (r + 1) * ROW_PIECE)
                p = jnp.exp(s_scr[buf][cc][rws, :] - shift)
                p_scr[cc][rws, :] = p.astype(BF16)
                psum = p if psum is None else psum + p
            alpha = jnp.exp(m_old - m_new)
            l_scr[cc][...] = alpha * l_scr[cc][...] + jnp.sum(psum, axis=0, keepdims=True)
            acc_scr[cc][...] = acc_scr[cc][...] * alpha + jnp.dot(vt, p_scr[cc][0:nk, :],
                                                                  preferred_element_type=F32)
            m_scr[cc][...] = m_new

    n_far = nkb - n_near
    lo = 0 if meta_queries else jnp.clip(qi - n_near // 2, 0, nkb - n_near)
    tm = 0 if meta_queries else jnp.minimum(qi, 1)
    key_blk = lambda kb: k_ref[0, pl.ds(pl.multiple_of(kb * bk, bk), bk), :]
    far_kb = lambda j: jnp.where(j < lo, j, j + n_near)

    def issue_logits(t, buf):
        if isinstance(t, int) and t == nkb:
            logits(k_ref[0, seq:seq + META_BLK, :], buf, btm_ref.at[0, tm])
        elif isinstance(t, int) and t >= n_far:
            kb = lo + (t - n_far)
            logits(key_blk(kb), buf, bt_ref.at[0, 0 if meta_queries else kb - qi + n_near - 1])
        else:
            logits(key_blk(far_kb(t)), buf)

    def finish(t, buf):
        if isinstance(t, int) and t == nkb:
            softmax_pv(vtm_scr[...], 0.0, buf)
        elif isinstance(t, int) and t >= n_far:
            softmax_pv(vt_scr[lo + (t - n_far)], 0.0, buf)
        else:
            softmax_pv(vt_scr[far_kb(t)], jnp.where(t < lo, c_neg, c_pos), buf)

    issue_logits(0, 0)
    n_group = (n_far - 1) // LOGIT_BUFS

    def group_body(i, carry):
        for u in range(LOGIT_BUFS):
            issue_logits(LOGIT_BUFS * i + u + 1, (u + 1) % LOGIT_BUFS)
            finish(LOGIT_BUFS * i + u, u)
        return carry

    lax.fori_loop(0, n_group, group_body, 0)
    for t in range(LOGIT_BUFS * n_group, nkb + 1):
        if t < nkb:
            issue_logits(t + 1, (t + 1) % LOGIT_BUFS)
        finish(t, t % LOGIT_BUFS)

    lp = lamp_ref[...]
    lam = (jnp.exp(jnp.sum(lp[0:1] * lp[1:2], axis=1, keepdims=True))
           - jnp.exp(jnp.sum(lp[2:3] * lp[3:4], axis=1, keepdims=True)) + lambda_init)
    on = jnp.concatenate([acc_scr[cc][...] / l_scr[cc][...] for cc in range(n_chunk)], axis=1)
    ot = on[:, :bq] - lam * on[:, bq:]
    y = ot * lax.rsqrt(jnp.mean(ot * ot, axis=0, keepdims=True) + RMS_EPS) * g_ref[...]
    o_ref[0] = (y * (1.0 - lambda_init)).T.astype(BF16)


def _t5_bucket(rel):
    nb = T5_BUCKETS // 2
    max_exact = nb // 2
    ret = jnp.where(rel > 0, nb, 0)
    n = jnp.abs(rel)
    nf = jnp.maximum(n, 1).astype(jnp.float32)
    large = max_exact + (jnp.log(nf / max_exact) / math.log(T5_MAX_DIST / max_exact)
                         * (nb - max_exact)).astype(jnp.int32)
    large = jnp.minimum(large, nb - 1)
    return ret + jnp.where(n < max_exact, n, large)


def _da_tables(rel_table, seq):
    blk = ATT_BLK
    n_head = rel_table.shape[1]
    reach = 2 * blk - 1
    line = rel_table[_t5_bucket(jnp.arange(-reach, reach + 1, dtype=jnp.int32))].T.astype(F32)
    val = lambda rel: line[:, np.clip(np.asarray(rel) + reach, 0, 2 * reach)]
    cst = jnp.stack([rel_table[T5_BUCKETS // 2 - 1], rel_table[T5_BUCKETS - 1]], axis=1).astype(F32)
    qq = np.arange(blk)[None, :]
    period = 2 * blk
    off = np.arange(period)
    off = np.where(off < blk, off, off - period)
    flat = lambda col: jnp.broadcast_to(cst[:, col, None, None], (n_head, blk, blk))
    bt = jnp.stack([flat(0)] + [_toeplitz(val(d * blk - off), blk, blk) for d in (-1, 0, 1)] + [flat(1)],
                   axis=1)
    jm = np.arange(N_META)[:, None]
    first = val(jm - (qq + N_META))
    later = jnp.broadcast_to(cst[:, 0, None, None], first.shape)
    pad_keys = lambda cols, n: jnp.full((n_head, n, META_BLK - N_META, cols), NEG, F32)
    btm = jnp.concatenate([jnp.stack([first, later], axis=1), pad_keys(blk, 2)], axis=2)
    pad_cols = lambda t: jnp.concatenate(
        [t, jnp.broadcast_to(t[..., -1:], t.shape[:-1] + (META_BLK - N_META,))], axis=-1)
    im = np.arange(N_META)[None, :]
    bt_q = pad_cols(val(np.arange(blk)[:, None] + N_META - im))[:, None]
    btm_q = jnp.concatenate([pad_cols(val(jm - im))[:, None], pad_keys(META_BLK, 1)], axis=2)
    return cst, bt, btm, bt_q, btm_q


def _da_call(qkv, prev_out, cst, bt, btm, lam_p, g_col, *, seq, meta_queries, lambda_init):
    b, tp, _ = qkv.shape
    nkb = seq // ATT_BLK
    bq = META_BLK if meta_queries else ATT_BLK
    nq = 1 if meta_queries else nkb
    q_row0 = seq // bq if meta_queries else 0
    n_t, n_tm = bt.shape[1], btm.shape[1]
    in_specs = [
        pl.BlockSpec(memory_space=pltpu.SMEM),
        pl.BlockSpec((1, bq, LANES), lambda i, h, j: (i, q_row0 + j, h)),
        pl.BlockSpec((1, tp, LANES), lambda i, h, j: (i, 0, DA_HEADS + h)),
        pl.BlockSpec((1, tp, LANES), lambda i, h, j: (i, 0, 2 * DA_HEADS + h)),
        pl.BlockSpec((1, n_t, ATT_BLK, bq), lambda i, h, j: (h, 0, 0, 0)),
        pl.BlockSpec((1, n_tm, META_BLK, bq), lambda i, h, j: (h, 0, 0, 0)),
        pl.BlockSpec((4, DA_HEAD_DIM), lambda i, h, j: (0, 0)),
        pl.BlockSpec((LANES, 1), lambda i, h, j: (0, 0)),
    ]
    args = [cst, qkv, qkv, qkv, bt, btm, lam_p, g_col]
    aliases = {}
    if prev_out is not None:
        in_specs.append(pl.BlockSpec(memory_space=pl.ANY))
        args.append(prev_out)
        aliases = {len(args) - 1: 0}
    kern = functools.partial(_da_kernel, bq=bq, nkb=nkb, n_near=(n_t + 1) // 2, seq=seq,
                             meta_queries=meta_queries, lambda_init=lambda_init)
    if prev_out is not None:
        body = kern
        kern = lambda *refs: body(*refs[:8], *refs[9:])
    return pl.pallas_call(
        kern,
        out_shape=jax.ShapeDtypeStruct((b, tp, D_MODEL), BF16),
        grid=(b, DA_HEADS, nq),
        in_specs=in_specs,
        out_specs=pl.BlockSpec((1, bq, LANES), lambda i, h, j: (i, q_row0 + j, h)),
        scratch_shapes=[pltpu.VMEM((nkb, LANES, ATT_BLK), BF16), pltpu.VMEM((LANES, META_BLK), BF16),
                        pltpu.VMEM((2 * bq, LANES), BF16)]
        + [pltpu.VMEM(shape, dt)
           for shape, dt in ([((1, COL_CHUNK), F32), ((1, COL_CHUNK), F32), ((LANES, COL_CHUNK), F32),
                              ((ATT_BLK, COL_CHUNK), BF16)]
                             + [((ATT_BLK, COL_CHUNK), F32)] * LOGIT_BUFS + [((1, COL_CHUNK), F32)] * LOGIT_BUFS)
           for _ in range(2 * bq // COL_CHUNK)],
        input_output_aliases=aliases,
        compiler_params=_params(("parallel", "parallel", "arbitrary")),
        name="da_meta_queries" if meta_queries else "da_attention",
    )(*args)


def kernel(x, meta_tokens, norm_g, ffn_w_gate, ffn_w_up, ffn_w_down, na_w_qkv, na_b_qkv, na_w_o, na_b_o, na_rpb, na_meta_bias, da_w_qkv, da_w_o, da_lambda, da_subln_g, t5_rel_bias):
    b, seq, d = x.shape
    depth = norm_g.shape[0]
    assert d == D_MODEL and seq % ATT_BLK == 0 and seq % GRID_W == 0
    tp = seq + META_BLK
    assert (b * tp) % ROW_TILE == 0
    meta = jnp.broadcast_to(meta_tokens[None].astype(x.dtype), (b, N_META, d))
    pad = jnp.zeros((b, META_BLK - N_META, d), x.dtype)
    h = jnp.concatenate([x, meta, pad], axis=1).reshape(b * tp, d)

    vec = lambda v: v.reshape(1, -1).astype(F32)
    zero_qkv_bias = jnp.zeros((1, 3 * d), F32)
    zero_o_bias = jnp.zeros((1, d), F32)
    da_cst = _da_tables(t5_rel_bias, seq)

    for i in range(depth):
        g = norm_g[i]
        j = i // N_MIXERS
        h = _ffn(h, vec(g[0]), vec(g[1]), ffn_w_gate[i, 0].astype(BF16), ffn_w_up[i, 0].astype(BF16),
                 ffn_w_down[i, 0].astype(BF16))
        if i % N_MIXERS == 0:
            qkv = _qkv(h, vec(g[2]), na_w_qkv[j].astype(BF16), vec(na_b_qkv[j]), (D_MODEL // NA_HEADS) ** -0.5)
            nab, mb = _na_tables(na_rpb[j], na_meta_bias[j], seq // GRID_W)
            a = _na_attention(qkv.reshape(b, tp, 3 * d), nab, mb, seq)
            h = _oproj(a.reshape(b * tp, d), h, na_w_o[j].astype(BF16), vec(na_b_o[j]), vec(g[3]))
        else:
            lambda_init = 0.8 - 0.6 * math.exp(-0.3 * i)
            qkv = _qkv(h, vec(g[2]), da_w_qkv[j].astype(BF16), zero_qkv_bias, DA_HEAD_DIM ** -0.5)
            qkv = qkv.reshape(b, tp, 3 * d)
            cst, bt, btm, bt_q, btm_q = da_cst
            g_col = da_subln_g[j].reshape(LANES, 1).astype(F32)
            lam_p = da_lambda[j].astype(F32)
            a = _da_call(qkv, None, cst, bt, btm, lam_p, g_col, seq=seq, meta_queries=False,
                         lambda_init=lambda_init)
            a = _da_call(qkv, a, cst, bt_q, btm_q, lam_p, g_col, seq=seq, meta_queries=True,
                         lambda_init=lambda_init)
            h = _oproj(a.reshape(b * tp, d), h, da_w_o[j].astype(BF16), zero_o_bias, vec(g[3]))
        h = _ffn(h, vec(g[4]), vec(g[5]), ffn_w_gate[i, 1].astype(BF16), ffn_w_up[i, 1].astype(BF16),
                 ffn_w_down[i, 1].astype(BF16))
    return h.reshape(b, tp, d)[:, :seq]
```

```python
import functools
import math

import numpy as np
import jax
import jax.numpy as jnp
from jax import lax
from jax.experimental import pallas as pl
from jax.experimental.pallas import tpu as pltpu

F32 = jnp.float32
BF16 = jnp.bfloat16

D_MODEL = 1024
N_META = 16
GRID_W = 64
NA_KH = 8
NA_KW = 16
NA_HEADS = 16
DA_HEADS = 8
DA_HEAD_DIM = 64
T5_BUCKETS = 32
T5_MAX_DIST = 128
D_FF = 2816
FFN_RES = 0.5
RMS_EPS = 1e-6
N_MIXERS = 2

LANES = 128
META_BLK = 128
ROW_TILE = 512
ATT_BLK = 512
COL_CHUNK = 256
ROW_PIECE = 64
MXU_DEPTH = 256
LOGIT_BUFS = 3
ONES_ROWS = 16
LOG2E = math.log2(math.e)
NEG = -1e30
VMEM_LIMIT = 56 * 1024 * 1024


def _rms(x, g):
    return x * lax.rsqrt(jnp.mean(x * x, axis=-1, keepdims=True) + RMS_EPS) * g


def _params(sem):
    return pltpu.CompilerParams(dimension_semantics=sem, vmem_limit_bytes=VMEM_LIMIT)


def _resident(shape):
    return pl.BlockSpec(shape, lambda *_: (0,) * len(shape), pipeline_mode=pl.Buffered(1))


def _ffn_kernel(h_ref, gin_ref, gout_ref, wg_ref, wu_ref, wd_ref, o_ref, *, fc):
    x = h_ref[...]
    xn = _rms(x, gin_ref[...]).astype(BF16)
    acc = None
    for c in range(D_FF // fc):
        sl = slice(c * fc, (c + 1) * fc)
        g = jnp.dot(xn, wg_ref[:, sl], preferred_element_type=F32)
        u = jnp.dot(xn, wu_ref[:, sl], preferred_element_type=F32)
        hid = (g * jax.nn.sigmoid(g) * u).astype(BF16)
        d = jnp.dot(hid, wd_ref[sl, :], preferred_element_type=F32)
        acc = d if acc is None else acc + d
    o_ref[...] = x + FFN_RES * _rms(acc, gout_ref[...])


def _ffn(h, g_in, g_out, wg, wu, wd):
    mt = h.shape[0]
    row = pl.BlockSpec((ROW_TILE, D_MODEL), lambda i: (i, 0))
    return pl.pallas_call(
        functools.partial(_ffn_kernel, fc=1408),
        out_shape=jax.ShapeDtypeStruct((mt, D_MODEL), F32),
        grid=(mt // ROW_TILE,),
        in_specs=[row, _resident((1, D_MODEL)), _resident((1, D_MODEL)),
                  _resident((D_MODEL, D_FF)), _resident((D_MODEL, D_FF)), _resident((D_FF, D_MODEL))],
        out_specs=row,
        compiler_params=_params(("parallel",)),
        name="ffn",
    )(h, g_in, g_out, wg, wu, wd)


def _qkv_kernel(h_ref, g_ref, w_ref, b_ref, o_ref, *, q_scale):
    xn = _rms(h_ref[...], g_ref[...]).astype(BF16)
    for j in range(3):
        sl = slice(j * D_MODEL, (j + 1) * D_MODEL)
        y = jnp.dot(xn, w_ref[:, sl], preferred_element_type=F32) + b_ref[:, sl]
        if j == 0:
            y = y * q_scale
        o_ref[:, sl] = y.astype(BF16)


def _qkv(h, g, w, b, q_scale):
    mt = h.shape[0]
    return pl.pallas_call(
        functools.partial(_qkv_kernel, q_scale=q_scale),
        out_shape=jax.ShapeDtypeStruct((mt, 3 * D_MODEL), BF16),
        grid=(mt // ROW_TILE,),
        in_specs=[pl.BlockSpec((ROW_TILE, D_MODEL), lambda i: (i, 0)), _resident((1, D_MODEL)),
                  _resident((D_MODEL, 3 * D_MODEL)), _resident((1, 3 * D_MODEL))],
        out_specs=pl.BlockSpec((ROW_TILE, 3 * D_MODEL), lambda i: (i, 0)),
        compiler_params=_params(("parallel",)),
        name="qkv_proj",
    )(h, g, w, b)


def _oproj_kernel(a_ref, h_ref, w_ref, b_ref, g_ref, o_ref):
    m = jnp.dot(a_ref[...], w_ref[...], preferred_element_type=F32) + b_ref[...]
    o_ref[...] = h_ref[...] + _rms(m, g_ref[...])


def _oproj(a, h, w, b, g):
    mt = h.shape[0]
    row = pl.BlockSpec((ROW_TILE, D_MODEL), lambda i: (i, 0))
    return pl.pallas_call(
        _oproj_kernel,
        out_shape=jax.ShapeDtypeStruct((mt, D_MODEL), F32),
        grid=(mt // ROW_TILE,),
        in_specs=[row, row, _resident((D_MODEL, D_MODEL)), _resident((1, D_MODEL)), _resident((1, D_MODEL))],
        out_specs=row,
        compiler_params=_params(("parallel",)),
        name="out_proj",
    )(a, h, w, b, g)


def _na_kernel(q_ref, k_ref, v_ref, nab_ref, mb_ref, o_ref, *, rows, seq):
    kh = NA_KH
    lane = lax.broadcasted_iota(jnp.int32, (GRID_W, LANES), 1)
    head0 = lane < (LANES // 2)
    k_meta = k_ref[0, seq:seq + META_BLK, :]
    v_meta = v_ref[0, seq:seq + META_BLK, :]
    mb = jnp.concatenate([jnp.broadcast_to(mb_ref[0, 0], (GRID_W, LANES)),
                          jnp.broadcast_to(mb_ref[0, 1], (GRID_W, LANES))], axis=0)
    nt = (((1,), (1,)), ((), ()))

    def split_heads(q):
        zero = jnp.zeros_like(q)
        return jnp.concatenate([jnp.where(head0[:q.shape[0]], q, zero),
                                jnp.where(head0[:q.shape[0]], zero, q)], axis=0)

    def row_fn(r, carry):
        rs = jnp.clip(r - kh // 2, 0, rows - kh)
        dy0 = rs - r + (NA_KH - 1)
        q2 = split_heads(q_ref[0, pl.ds(pl.multiple_of(r * GRID_W, GRID_W), GRID_W), :])
        kbase = pl.multiple_of(rs * GRID_W, GRID_W)
        k_win = k_ref[0, pl.ds(kbase, kh * GRID_W), :]
        v_win = v_ref[0, pl.ds(kbase, kh * GRID_W), :]
        bias = jnp.concatenate([nab_ref[0, 0, dy0], nab_ref[0, 1, dy0]], axis=0)
        s = lax.dot_general(q2, k_win, nt, preferred_element_type=F32) + bias
        sm = lax.dot_general(q2, k_meta, nt, preferred_element_type=F32) + mb
        m = jnp.maximum(jnp.max(s, axis=-1, keepdims=True), jnp.max(sm, axis=-1, keepdims=True))
        p = jnp.exp(s - m)
        pm = jnp.exp(sm - m)
        l = jnp.sum(p, axis=-1, keepdims=True) + jnp.sum(pm, axis=-1, keepdims=True)
        o2 = (jnp.dot(p.astype(BF16), v_win, preferred_element_type=F32)
              + jnp.dot(pm.astype(BF16), v_meta, preferred_element_type=F32)) / l
        o = jnp.where(head0, o2[:GRID_W], o2[GRID_W:])
        o_ref[0, pl.ds(pl.multiple_of(r * GRID_W, GRID_W), GRID_W), :] = o.astype(BF16)
        return carry

    lax.fori_loop(0, rows, row_fn, 0)

    qm = q_ref[0, seq:seq + META_BLK, :]
    lane_m = lax.broadcasted_iota(jnp.int32, (META_BLK, LANES), 1) < (LANES // 2)
    zero = jnp.zeros_like(qm)
    q2 = jnp.concatenate([jnp.where(lane_m, qm, zero), jnp.where(lane_m, zero, qm)], axis=0)
    mbm = jnp.concatenate([jnp.broadcast_to(mb_ref[0, 0], (META_BLK, LANES)),
                           jnp.broadcast_to(mb_ref[0, 1], (META_BLK, LANES))], axis=0)
    sm = lax.dot_general(q2, k_meta, nt, preferred_element_type=F32) + mbm
    pm = jnp.exp(sm - jnp.max(sm, axis=-1, keepdims=True))
    o2 = jnp.dot(pm.astype(BF16), v_meta, preferred_element_type=F32) / jnp.sum(pm, axis=-1, keepdims=True)
    o_ref[0, seq:seq + META_BLK, :] = jnp.where(lane_m, o2[:META_BLK], o2[META_BLK:]).astype(BF16)


def _toeplitz(w, n, m):
    period = w.shape[-1]
    width = period - 1
    assert m <= width
    reps = -(-(n * width) // period)
    flat = jnp.tile(w, (1,) * (w.ndim - 1) + (reps,))[..., :n * width]
    return flat.reshape(w.shape[:-1] + (n, width))[..., :m]


def _na_tables(rpb, meta_bias, rows):
    kh = NA_KH
    qc = np.arange(GRID_W)[:, None]
    kc = np.arange(GRID_W)[None, :]
    cs = np.clip(qc - NA_KW // 2, 0, GRID_W - NA_KW)
    valid = (kc >= cs) & (kc < cs + NA_KW)
    period = 2 * GRID_W
    off = np.arange(period)
    off = np.where(off < GRID_W, off, off - period)
    w = rpb[:, :, np.clip(off + NA_KW - 1, 0, 2 * NA_KW - 2)]
    t = jnp.where(valid, _toeplitz(w, GRID_W, GRID_W), NEG)
    t = jnp.stack([t[:, dy0:dy0 + kh].transpose(0, 2, 1, 3) for dy0 in range(kh)], axis=1)
    t = t.reshape(NA_HEADS // 2, 2, kh, GRID_W, kh * GRID_W)
    mb = jnp.full((NA_HEADS, LANES), NEG, F32).at[:, :N_META].set(meta_bias)
    return t.astype(F32), mb.reshape(NA_HEADS // 2, 2, 1, LANES)


def _na_attention(qkv, nab, mb, seq):
    b, tp, _ = qkv.shape
    rows = seq // GRID_W
    n_slab = D_MODEL // LANES
    slab = lambda off: pl.BlockSpec((1, tp, LANES), lambda i, s: (i, 0, off + s))
    return pl.pallas_call(
        functools.partial(_na_kernel, rows=rows, seq=seq),
        out_shape=jax.ShapeDtypeStruct((b, tp, D_MODEL), BF16),
        grid=(b, n_slab),
        in_specs=[slab(0), slab(n_slab), slab(2 * n_slab),
                  pl.BlockSpec((1, 2, NA_KH, GRID_W, NA_KH * GRID_W), lambda i, s: (s, 0, 0, 0, 0)),
                  pl.BlockSpec((1, 2, 1, LANES), lambda i, s: (s, 0, 0, 0))],
        out_specs=slab(0),
        compiler_params=_params(("parallel", "parallel")),
        name="na_attention",
    )(qkv, qkv, qkv, nab, mb)


def _da_kernel(cst_ref, q_ref, k_ref, v_ref, bt_ref, btm_ref, lamp_ref, g_ref, o_ref,
               vt_scr, vtm_scr, q2_scr, *chunk_scr, bq, nkb, n_near, seq, meta_queries, lambda_init):
    bk = ATT_BLK
    n_chunk = 2 * bq // COL_CHUNK
    groups = [chunk_scr[i * n_chunk:(i + 1) * n_chunk] for i in range(2 + 2 * LOGIT_BUFS)]
    m_scr, acc_scr = groups[:2]
    s_scr, smax_scr = groups[2:2 + LOGIT_BUFS], groups[2 + LOGIT_BUFS:]
    hd = pl.program_id(1)
    qi = pl.program_id(2)
    nt = (((1,), (1,)), ((), ()))

    @pl.when(qi == 0)
    def _():
        for kb in range(nkb):
            vt_scr[kb, 0:LANES, :] = v_ref[0, kb * bk:(kb + 1) * bk, :].astype(F32).T.astype(BF16)
            vt_scr[kb, LANES:, :] = jnp.ones((ONES_ROWS, bk), BF16)
        vtm_scr[0:LANES, :] = v_ref[0, seq:seq + META_BLK, :].astype(F32).T.astype(BF16)
        vtm_scr[LANES:, :] = jnp.ones((ONES_ROWS, META_BLK), BF16)

    q = q_ref[0]
    first = lax.broadcasted_iota(jnp.int32, q.shape, 1) < DA_HEAD_DIM
    zero = jnp.zeros_like(q)
    q2_scr[0:bq, :] = jnp.where(first, q, zero)
    q2_scr[bq:2 * bq, :] = jnp.where(first, zero, q)
    for cc in range(n_chunk):
        m_scr[cc][...] = jnp.full(m_scr[cc].shape, NEG, F32)
        acc_scr[cc][...] = jnp.zeros(acc_scr[cc].shape, F32)
    c_neg = cst_ref[hd, 0]
    c_pos = cst_ref[hd, 1]

    def logits(k_blk, buf, bias=None):
        nk = k_blk.shape[0]
        for cc in range(n_chunk):
            s = lax.dot_general(k_blk, q2_scr[cc * COL_CHUNK:(cc + 1) * COL_CHUNK, :], nt,
                                preferred_element_type=F32)
            if bias is not None:
                if bq >= COL_CHUNK:
                    b0 = (cc * COL_CHUNK) % bq
                    s = s + bias[:, b0:b0 + COL_CHUNK]
                else:
                    s = s + jnp.concatenate([bias[...]] * (COL_CHUNK // bq), axis=1)
            s_scr[buf][cc][0:nk, :] = s
            smax_scr[buf][cc][...] = jnp.max(s, axis=0, keepdims=True)

    def softmax_pv(vt, c, buf):
        nk = vt.shape[1]
        for cc in range(n_chunk):
            m_old = m_scr[cc][...]
            m_new = jnp.maximum(m_old, smax_scr[buf][cc][...] + c)
            shift = m_new - c
            acc = acc_scr[cc][...] * jnp.exp2(m_old - m_new)
            kd = min(nk, MXU_DEPTH)
            for k0 in range(0, nk, kd):
                p = jnp.concatenate(
                    [jnp.exp2(s_scr[buf][cc][r0:r0 + ROW_PIECE, :] - shift).astype(BF16)
                     for r0 in range(k0, k0 + kd, ROW_PIECE)], axis=0)
                acc = acc + jnp.dot(vt[:, k0:k0 + kd], p, preferred_element_type=F32)
            acc_scr[cc][...] = acc
            m_scr[cc][...] = m_new

    n_far = nkb - n_near
    lo = 0 if meta_queries else jnp.clip(qi - n_near // 2, 0, nkb - n_near)
    tm = 0 if meta_queries else jnp.minimum(qi, 1)
    key_blk = lambda kb: k_ref[0, pl.ds(pl.multiple_of(kb * bk, bk), bk), :]
    far_kb = lambda j: jnp.where(j < lo, j, j + n_near)

    def issue_logits(t, buf):
        if isinstance(t, int) and t == nkb:
            logits(k_ref[0, seq:seq + META_BLK, :], buf, btm_ref.at[0, tm])
        elif isinstance(t, int) and t >= n_far:
            kb = lo + (t - n_far)
            logits(key_blk(kb), buf, bt_ref.at[0, 0 if meta_queries else kb - qi + n_near - 1])
        else:
            logits(key_blk(far_kb(t)), buf)

    def finish(t, buf):
        if isinstance(t, int) and t == nkb:
            softmax_pv(vtm_scr[...], 0.0, buf)
        elif isinstance(t, int) and t >= n_far:
            softmax_pv(vt_scr[lo + (t - n_far)], 0.0, buf)
        else:
            softmax_pv(vt_scr[far_kb(t)], jnp.where(t < lo, c_neg, c_pos), buf)

    issue_logits(0, 0)
    n_group = (n_far - 1) // LOGIT_BUFS

    def group_body(i, carry):
        for u in range(LOGIT_BUFS):
            issue_logits(LOGIT_BUFS * i + u + 1, (u + 1) % LOGIT_BUFS)
            finish(LOGIT_BUFS * i + u, u)
        return carry

    lax.fori_loop(0, n_group, group_body, 0)
    for t in range(LOGIT_BUFS * n_group, nkb + 1):
        if t < nkb:
            issue_logits(t + 1, (t + 1) % LOGIT_BUFS)
        finish(t, t % LOGIT_BUFS)

    lp = lamp_ref[...]
    lam = (jnp.exp(jnp.sum(lp[0:1] * lp[1:2], axis=1, keepdims=True))
           - jnp.exp(jnp.sum(lp[2:3] * lp[3:4], axis=1, keepdims=True)) + lambda_init)
    on = jnp.concatenate([acc_scr[cc][0:LANES, :] / acc_scr[cc][LANES:LANES + 1, :] for cc in range(n_chunk)],
                         axis=1)
    ot = on[:, :bq] - lam * on[:, bq:]
    y = ot * lax.rsqrt(jnp.mean(ot * ot, axis=0, keepdims=True) + RMS_EPS) * g_ref[...]
    o_ref[0] = (y * (1.0 - lambda_init)).T.astype(BF16)


def _t5_bucket(rel):
    nb = T5_BUCKETS // 2
    max_exact = nb // 2
    ret = jnp.where(rel > 0, nb, 0)
    n = jnp.abs(rel)
    nf = jnp.maximum(n, 1).astype(jnp.float32)
    large = max_exact + (jnp.log(nf / max_exact) / math.log(T5_MAX_DIST / max_exact)
                         * (nb - max_exact)).astype(jnp.int32)
    large = jnp.minimum(large, nb - 1)
    return ret + jnp.where(n < max_exact, n, large)


def _da_tables(rel_table, seq):
    blk = ATT_BLK
    n_head = rel_table.shape[1]
    reach = 2 * blk - 1
    rel_table = rel_table.astype(F32) * LOG2E
    line = rel_table[_t5_bucket(jnp.arange(-reach, reach + 1, dtype=jnp.int32))].T
    val = lambda rel: line[:, np.clip(np.asarray(rel) + reach, 0, 2 * reach)]
    cst = jnp.stack([rel_table[T5_BUCKETS // 2 - 1], rel_table[T5_BUCKETS - 1]], axis=1)
    qq = np.arange(blk)[None, :]
    period = 2 * blk
    off = np.arange(period)
    off = np.where(off < blk, off, off - period)
    flat = lambda col: jnp.broadcast_to(cst[:, col, None, None], (n_head, blk, blk))
    bt = jnp.stack([flat(0)] + [_toeplitz(val(d * blk - off), blk, blk) for d in (-1, 0, 1)] + [flat(1)],
                   axis=1)
    jm = np.arange(N_META)[:, None]
    first = val(jm - (qq + N_META))
    later = jnp.broadcast_to(cst[:, 0, None, None], first.shape)
    pad_keys = lambda cols, n: jnp.full((n_head, n, META_BLK - N_META, cols), NEG, F32)
    btm = jnp.concatenate([jnp.stack([first, later], axis=1), pad_keys(blk, 2)], axis=2)
    pad_cols = lambda t: jnp.concatenate(
        [t, jnp.broadcast_to(t[..., -1:], t.shape[:-1] + (META_BLK - N_META,))], axis=-1)
    im = np.arange(N_META)[None, :]
    bt_q = pad_cols(val(np.arange(blk)[:, None] + N_META - im))[:, None]
    btm_q = jnp.concatenate([pad_cols(val(jm - im))[:, None], pad_keys(META_BLK, 1)], axis=2)
    return cst, bt, btm, bt_q, btm_q


def _da_call(qkv, prev_out, cst, bt, btm, lam_p, g_col, *, seq, meta_queries, lambda_init):
    b, tp, _ = qkv.shape
    nkb = seq // ATT_BLK
    bq = META_BLK if meta_queries else ATT_BLK
    nq = 1 if meta_queries else nkb
    q_row0 = seq // bq if meta_queries else 0
    n_t, n_tm = bt.shape[1], btm.shape[1]
    in_specs = [
        pl.BlockSpec(memory_space=pltpu.SMEM),
        pl.BlockSpec((1, bq, LANES), lambda i, h, j: (i, q_row0 + j, h)),
        pl.BlockSpec((1, tp, LANES), lambda i, h, j: (i, 0, DA_HEADS + h)),
        pl.BlockSpec((1, tp, LANES), lambda i, h, j: (i, 0, 2 * DA_HEADS + h)),
        pl.BlockSpec((1, n_t, ATT_BLK, bq), lambda i, h, j: (h, 0, 0, 0)),
        pl.BlockSpec((1, n_tm, META_BLK, bq), lambda i, h, j: (h, 0, 0, 0)),
        pl.BlockSpec((4, DA_HEAD_DIM), lambda i, h, j: (0, 0)),
        pl.BlockSpec((LANES, 1), lambda i, h, j: (0, 0)),
    ]
    args = [cst, qkv, qkv, qkv, bt, btm, lam_p, g_col]
    aliases = {}
    if prev_out is not None:
        in_specs.append(pl.BlockSpec(memory_space=pl.ANY))
        args.append(prev_out)
        aliases = {len(args) - 1: 0}
    kern = functools.partial(_da_kernel, bq=bq, nkb=nkb, n_near=(n_t + 1) // 2, seq=seq,
                             meta_queries=meta_queries, lambda_init=lambda_init)
    if prev_out is not None:
        body = kern
        kern = lambda *refs: body(*refs[:8], *refs[9:])
    return pl.pallas_call(
        kern,
        out_shape=jax.ShapeDtypeStruct((b, tp, D_MODEL), BF16),
        grid=(b, DA_HEADS, nq),
        in_specs=in_specs,
        out_specs=pl.BlockSpec((1, bq, LANES), lambda i, h, j: (i, q_row0 + j, h)),
        scratch_shapes=[pltpu.VMEM((nkb, LANES + ONES_ROWS, ATT_BLK), BF16),
                        pltpu.VMEM((LANES + ONES_ROWS, META_BLK), BF16), pltpu.VMEM((2 * bq, LANES), BF16)]
        + [pltpu.VMEM(shape, dt)
           for shape, dt in ([((1, COL_CHUNK), F32), ((LANES + ONES_ROWS, COL_CHUNK), F32)]
                             + [((ATT_BLK, COL_CHUNK), F32)] * LOGIT_BUFS + [((1, COL_CHUNK), F32)] * LOGIT_BUFS)
           for _ in range(2 * bq // COL_CHUNK)],
        input_output_aliases=aliases,
        compiler_params=_params(("parallel", "parallel", "arbitrary")),
        name="da_meta_queries" if meta_queries else "da_attention",
    )(*args)


def kernel(x, meta_tokens, norm_g, ffn_w_gate, ffn_w_up, ffn_w_down, na_w_qkv, na_b_qkv, na_w_o, na_b_o, na_rpb, na_meta_bias, da_w_qkv, da_w_o, da_lambda, da_subln_g, t5_rel_bias):
    b, seq, d = x.shape
    depth = norm_g.shape[0]
    assert d == D_MODEL and seq % ATT_BLK == 0 and seq % GRID_W == 0
    tp = seq + META_BLK
    assert (b * tp) % ROW_TILE == 0
    meta = jnp.broadcast_to(meta_tokens[None].astype(x.dtype), (b, N_META, d))
    pad = jnp.zeros((b, META_BLK - N_META, d), x.dtype)
    h = jnp.concatenate([x, meta, pad], axis=1).reshape(b * tp, d)

    vec = lambda v: v.reshape(1, -1).astype(F32)
    zero_qkv_bias = jnp.zeros((1, 3 * d), F32)
    zero_o_bias = jnp.zeros((1, d), F32)
    da_cst = _da_tables(t5_rel_bias, seq)

    for i in range(depth):
        g = norm_g[i]
        j = i // N_MIXERS
        h = _ffn(h, vec(g[0]), vec(g[1]), ffn_w_gate[i, 0].astype(BF16), ffn_w_up[i, 0].astype(BF16),
                 ffn_w_down[i, 0].astype(BF16))
        if i % N_MIXERS == 0:
            qkv = _qkv(h, vec(g[2]), na_w_qkv[j].astype(BF16), vec(na_b_qkv[j]), (D_MODEL // NA_HEADS) ** -0.5)
            nab, mb = _na_tables(na_rpb[j], na_meta_bias[j], seq // GRID_W)
            a = _na_attention(qkv.reshape(b, tp, 3 * d), nab, mb, seq)
            h = _oproj(a.reshape(b * tp, d), h, na_w_o[j].astype(BF16), vec(na_b_o[j]), vec(g[3]))
        else:
            lambda_init = 0.8 - 0.6 * math.exp(-0.3 * i)
            qkv = _qkv(h, vec(g[2]), da_w_qkv[j].astype(BF16), zero_qkv_bias, DA_HEAD_DIM ** -0.5 * LOG2E)
            qkv = qkv.reshape(b, tp, 3 * d)
            cst, bt, btm, bt_q, btm_q = da_cst
            g_col = da_subln_g[j].reshape(LANES, 1).astype(F32)
            lam_p = da_lambda[j].astype(F32)
            a = _da_call(qkv, None, cst, bt, btm, lam_p, g_col, seq=seq, meta_queries=False,
                         lambda_init=lambda_init)
            a = _da_call(qkv, a, cst, bt_q, btm_q, lam_p, g_col, seq=seq, meta_queries=True,
                         lambda_init=lambda_init)
            h = _oproj(a.reshape(b * tp, d), h, da_w_o[j].astype(BF16), zero_o_bias, vec(g[3]))
        h = _ffn(h, vec(g[4]), vec(g[5]), ffn_w_gate[i, 1].astype(BF16), ffn_w_up[i, 1].astype(BF16),
                 ffn_w_down[i, 1].astype(BF16))
    return h.reshape(b, tp, d)[:, :seq]
```

```python
import functools
import math

import numpy as np
import jax
import jax.numpy as jnp
from jax import lax
from jax.experimental import pallas as pl
from jax.experimental.pallas import tpu as pltpu

F32 = jnp.float32
BF16 = jnp.bfloat16

D_MODEL = 1024
N_META = 16
GRID_W = 64
NA_KH = 8
NA_KW = 16
NA_HEADS = 16
DA_HEADS = 8
DA_HEAD_DIM = 64
T5_BUCKETS = 32
T5_MAX_DIST = 128
D_FF = 2816
FFN_RES = 0.5
RMS_EPS = 1e-6
N_MIXERS = 2

LANES = 128
META_BLK = 128
ROW_TILE = 512
ATT_BLK = 512
COL_CHUNK = 256
ROW_PIECE = 64
NA_ROWS_PER_STEP = 8
MXU_DEPTH = 256
LOGIT_BUFS = 3
ONES_ROWS = 16
LOG2E = math.log2(math.e)
NEG = -1e30
VMEM_LIMIT = 56 * 1024 * 1024


def _rms(x, g):
    return x * lax.rsqrt(jnp.mean(x * x, axis=-1, keepdims=True) + RMS_EPS) * g


def _params(sem):
    return pltpu.CompilerParams(dimension_semantics=sem, vmem_limit_bytes=VMEM_LIMIT)


def _resident(shape):
    return pl.BlockSpec(shape, lambda *_: (0,) * len(shape), pipeline_mode=pl.Buffered(1))


def _ffn_kernel(h_ref, gin_ref, gout_ref, wg_ref, wu_ref, wd_ref, o_ref, *, fc):
    x = h_ref[...]
    xn = _rms(x, gin_ref[...]).astype(BF16)
    acc = None
    for c in range(D_FF // fc):
        sl = slice(c * fc, (c + 1) * fc)
        g = jnp.dot(xn, wg_ref[:, sl], preferred_element_type=F32)
        u = jnp.dot(xn, wu_ref[:, sl], preferred_element_type=F32)
        hid = (g * jax.nn.sigmoid(g) * u).astype(BF16)
        d = jnp.dot(hid, wd_ref[sl, :], preferred_element_type=F32)
        acc = d if acc is None else acc + d
    o_ref[...] = x + FFN_RES * _rms(acc, gout_ref[...])


def _ffn(h, g_in, g_out, wg, wu, wd):
    mt = h.shape[0]
    row = pl.BlockSpec((ROW_TILE, D_MODEL), lambda i: (i, 0))
    return pl.pallas_call(
        functools.partial(_ffn_kernel, fc=1408),
        out_shape=jax.ShapeDtypeStruct((mt, D_MODEL), F32),
        grid=(mt // ROW_TILE,),
        in_specs=[row, _resident((1, D_MODEL)), _resident((1, D_MODEL)),
                  _resident((D_MODEL, D_FF)), _resident((D_MODEL, D_FF)), _resident((D_FF, D_MODEL))],
        out_specs=row,
        compiler_params=_params(("parallel",)),
        name="ffn",
    )(h, g_in, g_out, wg, wu, wd)


def _qkv_kernel(h_ref, g_ref, w_ref, b_ref, o_ref, *, q_scale):
    xn = _rms(h_ref[...], g_ref[...]).astype(BF16)
    for j in range(3):
        sl = slice(j * D_MODEL, (j + 1) * D_MODEL)
        y = jnp.dot(xn, w_ref[:, sl], preferred_element_type=F32) + b_ref[:, sl]
        if j == 0:
            y = y * q_scale
        o_ref[:, sl] = y.astype(BF16)


def _qkv(h, g, w, b, q_scale):
    mt = h.shape[0]
    return pl.pallas_call(
        functools.partial(_qkv_kernel, q_scale=q_scale),
        out_shape=jax.ShapeDtypeStruct((mt, 3 * D_MODEL), BF16),
        grid=(mt // ROW_TILE,),
        in_specs=[pl.BlockSpec((ROW_TILE, D_MODEL), lambda i: (i, 0)), _resident((1, D_MODEL)),
                  _resident((D_MODEL, 3 * D_MODEL)), _resident((1, 3 * D_MODEL))],
        out_specs=pl.BlockSpec((ROW_TILE, 3 * D_MODEL), lambda i: (i, 0)),
        compiler_params=_params(("parallel",)),
        name="qkv_proj",
    )(h, g, w, b)


def _oproj_kernel(a_ref, h_ref, w_ref, b_ref, g_ref, o_ref):
    m = jnp.dot(a_ref[...], w_ref[...], preferred_element_type=F32) + b_ref[...]
    o_ref[...] = h_ref[...] + _rms(m, g_ref[...])


def _oproj(a, h, w, b, g):
    mt = h.shape[0]
    row = pl.BlockSpec((ROW_TILE, D_MODEL), lambda i: (i, 0))
    return pl.pallas_call(
        _oproj_kernel,
        out_shape=jax.ShapeDtypeStruct((mt, D_MODEL), F32),
        grid=(mt // ROW_TILE,),
        in_specs=[row, row, _resident((D_MODEL, D_MODEL)), _resident((1, D_MODEL)), _resident((1, D_MODEL))],
        out_specs=row,
        compiler_params=_params(("parallel",)),
        name="out_proj",
    )(a, h, w, b, g)


def _na_kernel(q_ref, k_ref, v_ref, nab_ref, mb_ref, o_ref, *, rows, seq):
    kh = NA_KH
    lane = lax.broadcasted_iota(jnp.int32, (GRID_W, LANES), 1)
    head0 = lane < (LANES // 2)
    k_meta = k_ref[0, seq:seq + META_BLK, :]
    v_meta = v_ref[0, seq:seq + META_BLK, :]
    mb = jnp.concatenate([jnp.broadcast_to(mb_ref[0, 0], (GRID_W, LANES)),
                          jnp.broadcast_to(mb_ref[0, 1], (GRID_W, LANES))], axis=0)
    nt = (((1,), (1,)), ((), ()))

    def split_heads(q):
        zero = jnp.zeros_like(q)
        return jnp.concatenate([jnp.where(head0[:q.shape[0]], q, zero),
                                jnp.where(head0[:q.shape[0]], zero, q)], axis=0)

    def row_fn(r):
        rs = jnp.clip(r - kh // 2, 0, rows - kh)
        dy0 = rs - r + (NA_KH - 1)
        q2 = split_heads(q_ref[0, pl.ds(pl.multiple_of(r * GRID_W, GRID_W), GRID_W), :])
        kbase = pl.multiple_of(rs * GRID_W, GRID_W)
        k_win = k_ref[0, pl.ds(kbase, kh * GRID_W), :]
        v_win = v_ref[0, pl.ds(kbase, kh * GRID_W), :]
        bias = jnp.concatenate([nab_ref[0, 0, dy0], nab_ref[0, 1, dy0]], axis=0)
        s = lax.dot_general(q2, k_win, nt, preferred_element_type=F32) + bias
        sm = lax.dot_general(q2, k_meta, nt, preferred_element_type=F32) + mb
        m = jnp.maximum(jnp.max(s, axis=-1, keepdims=True), jnp.max(sm, axis=-1, keepdims=True))
        p = jnp.exp(s - m)
        pm = jnp.exp(sm - m)
        l = jnp.sum(p, axis=-1, keepdims=True) + jnp.sum(pm, axis=-1, keepdims=True)
        o2 = (jnp.dot(p.astype(BF16), v_win, preferred_element_type=F32)
              + jnp.dot(pm.astype(BF16), v_meta, preferred_element_type=F32)) / l
        o = jnp.where(head0, o2[:GRID_W], o2[GRID_W:])
        o_ref[0, pl.ds(pl.multiple_of(r * GRID_W, GRID_W), GRID_W), :] = o.astype(BF16)

    def row_group(i, carry):
        for u in range(NA_ROWS_PER_STEP):
            row_fn(i * NA_ROWS_PER_STEP + u)
        return carry

    lax.fori_loop(0, rows // NA_ROWS_PER_STEP, row_group, 0)

    qm = q_ref[0, seq:seq + META_BLK, :]
    lane_m = lax.broadcasted_iota(jnp.int32, (META_BLK, LANES), 1) < (LANES // 2)
    zero = jnp.zeros_like(qm)
    q2 = jnp.concatenate([jnp.where(lane_m, qm, zero), jnp.where(lane_m, zero, qm)], axis=0)
    mbm = jnp.concatenate([jnp.broadcast_to(mb_ref[0, 0], (META_BLK, LANES)),
                           jnp.broadcast_to(mb_ref[0, 1], (META_BLK, LANES))], axis=0)
    sm = lax.dot_general(q2, k_meta, nt, preferred_element_type=F32) + mbm
    pm = jnp.exp(sm - jnp.max(sm, axis=-1, keepdims=True))
    o2 = jnp.dot(pm.astype(BF16), v_meta, preferred_element_type=F32) / jnp.sum(pm, axis=-1, keepdims=True)
    o_ref[0, seq:seq + META_BLK, :] = jnp.where(lane_m, o2[:META_BLK], o2[META_BLK:]).astype(BF16)


def _toeplitz(w, n, m):
    period = w.shape[-1]
    width = period - 1
    assert m <= width
    reps = -(-(n * width) // period)
    flat = jnp.tile(w, (1,) * (w.ndim - 1) + (reps,))[..., :n * width]
    return flat.reshape(w.shape[:-1] + (n, width))[..., :m]


def _na_tables(rpb, meta_bias, rows):
    kh = NA_KH
    qc = np.arange(GRID_W)[:, None]
    kc = np.arange(GRID_W)[None, :]
    cs = np.clip(qc - NA_KW // 2, 0, GRID_W - NA_KW)
    valid = (kc >= cs) & (kc < cs + NA_KW)
    period = 2 * GRID_W
    off = np.arange(period)
    off = np.where(off < GRID_W, off, off - period)
    w = rpb[:, :, np.clip(off + NA_KW - 1, 0, 2 * NA_KW - 2)]
    t = jnp.where(valid, _toeplitz(w, GRID_W, GRID_W), NEG)
    t = jnp.stack([t[:, dy0:dy0 + kh].transpose(0, 2, 1, 3) for dy0 in range(kh)], axis=1)
    t = t.reshape(NA_HEADS // 2, 2, kh, GRID_W, kh * GRID_W)
    mb = jnp.full((NA_HEADS, LANES), NEG, F32).at[:, :N_META].set(meta_bias)
    return t.astype(F32), mb.reshape(NA_HEADS // 2, 2, 1, LANES)


def _na_attention(qkv, nab, mb, seq):
    b, tp, _ = qkv.shape
    rows = seq // GRID_W
    n_slab = D_MODEL // LANES
    slab = lambda off: pl.BlockSpec((1, tp, LANES), lambda i, s: (i, 0, off + s))
    return pl.pallas_call(
        functools.partial(_na_kernel, rows=rows, seq=seq),
        out_shape=jax.ShapeDtypeStruct((b, tp, D_MODEL), BF16),
        grid=(b, n_slab),
        in_specs=[slab(0), slab(n_slab), slab(2 * n_slab),
                  pl.BlockSpec((1, 2, NA_KH, GRID_W, NA_KH * GRID_W), lambda i, s: (s, 0, 0, 0, 0)),
                  pl.BlockSpec((1, 2, 1, LANES), lambda i, s: (s, 0, 0, 0))],
        out_specs=slab(0),
        compiler_params=_params(("parallel", "parallel")),
        name="na_attention",
    )(qkv, qkv, qkv, nab, mb)


def _da_kernel(cst_ref, q_ref, k_ref, v_ref, bt_ref, btm_ref, lamp_ref, g_ref, o_ref,
               vt_scr, vtm_scr, q2_scr, *chunk_scr, bq, nkb, n_near, seq, meta_queries, lambda_init):
    bk = ATT_BLK
    n_chunk = 2 * bq // COL_CHUNK
    groups = [chunk_scr[i * n_chunk:(i + 1) * n_chunk] for i in range(2 + 2 * LOGIT_BUFS)]
    m_scr, acc_scr = groups[:2]
    s_scr, smax_scr = groups[2:2 + LOGIT_BUFS], groups[2 + LOGIT_BUFS:]
    hd = pl.program_id(1)
    qi = pl.program_id(2)
    nt = (((1,), (1,)), ((), ()))

    @pl.when(qi == 0)
    def _():
        for kb in range(nkb):
            vt_scr[kb, 0:LANES, :] = v_ref[0, kb * bk:(kb + 1) * bk, :].astype(F32).T.astype(BF16)
            vt_scr[kb, LANES:, :] = jnp.ones((ONES_ROWS, bk), BF16)
        vtm_scr[0:LANES, :] = v_ref[0, seq:seq + META_BLK, :].astype(F32).T.astype(BF16)
        vtm_scr[LANES:, :] = jnp.ones((ONES_ROWS, META_BLK), BF16)

    q = q_ref[0]
    first = lax.broadcasted_iota(jnp.int32, q.shape, 1) < DA_HEAD_DIM
    zero = jnp.zeros_like(q)
    q2_scr[0:bq, :] = jnp.where(first, q, zero)
    q2_scr[bq:2 * bq, :] = jnp.where(first, zero, q)
    for cc in range(n_chunk):
        m_scr[cc][...] = jnp.full(m_scr[cc].shape, NEG, F32)
        acc_scr[cc][...] = jnp.zeros(acc_scr[cc].shape, F32)
    c_neg = cst_ref[hd, 0]
    c_pos = cst_ref[hd, 1]

    def logits(k_blk, buf, bias=None):
        nk = k_blk.shape[0]
        for cc in range(n_chunk):
            s = lax.dot_general(k_blk, q2_scr[cc * COL_CHUNK:(cc + 1) * COL_CHUNK, :], nt,
                                preferred_element_type=F32)
            if bias is not None:
                if bq >= COL_CHUNK:
                    b0 = (cc * COL_CHUNK) % bq
                    s = s + bias[:, b0:b0 + COL_CHUNK]
                else:
                    s = s + jnp.concatenate([bias[...]] * (COL_CHUNK // bq), axis=1)
            s_scr[buf][cc][0:nk, :] = s
            smax_scr[buf][cc][...] = jnp.max(s, axis=0, keepdims=True)

    def softmax_pv(vt, c, buf):
        nk = vt.shape[1]
        for cc in range(n_chunk):
            m_old = m_scr[cc][...]
            m_new = jnp.maximum(m_old, smax_scr[buf][cc][...] + c)
            shift = m_new - c
            acc = acc_scr[cc][...] * jnp.exp2(m_old - m_new)
            kd = min(nk, MXU_DEPTH)
            for k0 in range(0, nk, kd):
                p = jnp.concatenate(
                    [jnp.exp2(s_scr[buf][cc][r0:r0 + ROW_PIECE, :] - shift).astype(BF16)
                     for r0 in range(k0, k0 + kd, ROW_PIECE)], axis=0)
                acc = acc + jnp.dot(vt[:, k0:k0 + kd], p, preferred_element_type=F32)
            acc_scr[cc][...] = acc
            m_scr[cc][...] = m_new

    n_far = nkb - n_near
    lo = 0 if meta_queries else jnp.clip(qi - n_near // 2, 0, nkb - n_near)
    tm = 0 if meta_queries else jnp.minimum(qi, 1)
    key_blk = lambda kb: k_ref[0, pl.ds(pl.multiple_of(kb * bk, bk), bk), :]
    far_kb = lambda j: jnp.where(j < lo, j, j + n_near)

    def issue_logits(t, buf):
        if isinstance(t, int) and t == nkb:
            logits(k_ref[0, seq:seq + META_BLK, :], buf, btm_ref.at[0, tm])
        elif isinstance(t, int) and t >= n_far:
            kb = lo + (t - n_far)
            logits(key_blk(kb), buf, bt_ref.at[0, 0 if meta_queries else kb - qi + n_near - 1])
        else:
            logits(key_blk(far_kb(t)), buf)

    def finish(t, buf):
        if isinstance(t, int) and t == nkb:
            softmax_pv(vtm_scr[...], 0.0, buf)
        elif isinstance(t, int) and t >= n_far:
            softmax_pv(vt_scr[lo + (t - n_far)], 0.0, buf)
        else:
            softmax_pv(vt_scr[far_kb(t)], jnp.where(t < lo, c_neg, c_pos), buf)

    issue_logits(0, 0)
    n_group = (n_far - 1) // LOGIT_BUFS

    def group_body(i, carry):
        for u in range(LOGIT_BUFS):
            issue_logits(LOGIT_BUFS * i + u + 1, (u + 1) % LOGIT_BUFS)
            finish(LOGIT_BUFS * i + u, u)
        return carry

    lax.fori_loop(0, n_group, group_body, 0)
    for t in range(LOGIT_BUFS * n_group, nkb + 1):
        if t < nkb:
            issue_logits(t + 1, (t + 1) % LOGIT_BUFS)
        finish(t, t % LOGIT_BUFS)

    lp = lamp_ref[...]
    lam = (jnp.exp(jnp.sum(lp[0:1] * lp[1:2], axis=1, keepdims=True))
           - jnp.exp(jnp.sum(lp[2:3] * lp[3:4], axis=1, keepdims=True)) + lambda_init)
    on = jnp.concatenate([acc_scr[cc][0:LANES, :] / acc_scr[cc][LANES:LANES + 1, :] for cc in range(n_chunk)],
                         axis=1)
    ot = on[:, :bq] - lam * on[:, bq:]
    y = ot * lax.rsqrt(jnp.mean(ot * ot, axis=0, keepdims=True) + RMS_EPS) * g_ref[...]
    o_ref[0] = (y * (1.0 - lambda_init)).T.astype(BF16)


def _t5_bucket(rel):
    nb = T5_BUCKETS // 2
    max_exact = nb // 2
    ret = jnp.where(rel > 0, nb, 0)
    n = jnp.abs(rel)
    nf = jnp.maximum(n, 1).astype(jnp.float32)
    large = max_exact + (jnp.log(nf / max_exact) / math.log(T5_MAX_DIST / max_exact)
                         * (nb - max_exact)).astype(jnp.int32)
    large = jnp.minimum(large, nb - 1)
    return ret + jnp.where(n < max_exact, n, large)


def _da_tables(rel_table, seq):
    blk = ATT_BLK
    n_head = rel_table.shape[1]
    reach = 2 * blk - 1
    rel_table = rel_table.astype(F32) * LOG2E
    line = rel_table[_t5_bucket(jnp.arange(-reach, reach + 1, dtype=jnp.int32))].T
    val = lambda rel: line[:, np.clip(np.asarray(rel) + reach, 0, 2 * reach)]
    cst = jnp.stack([rel_table[T5_BUCKETS // 2 - 1], rel_table[T5_BUCKETS - 1]], axis=1)
    qq = np.arange(blk)[None, :]
    period = 2 * blk
    off = np.arange(period)
    off = np.where(off < blk, off, off - period)
    flat = lambda col: jnp.broadcast_to(cst[:, col, None, None], (n_head, blk, blk))
    bt = jnp.stack([flat(0)] + [_toeplitz(val(d * blk - off), blk, blk) for d in (-1, 0, 1)] + [flat(1)],
                   axis=1)
    jm = np.arange(N_META)[:, None]
    first = val(jm - (qq + N_META))
    later = jnp.broadcast_to(cst[:, 0, None, None], first.shape)
    pad_keys = lambda cols, n: jnp.full((n_head, n, META_BLK - N_META, cols), NEG, F32)
    btm = jnp.concatenate([jnp.stack([first, later], axis=1), pad_keys(blk, 2)], axis=2)
    pad_cols = lambda t: jnp.concatenate(
        [t, jnp.broadcast_to(t[..., -1:], t.shape[:-1] + (META_BLK - N_META,))], axis=-1)
    im = np.arange(N_META)[None, :]
    bt_q = pad_cols(val(np.arange(blk)[:, None] + N_META - im))[:, None]
    btm_q = jnp.concatenate([pad_cols(val(jm - im))[:, None], pad_keys(META_BLK, 1)], axis=2)
    return cst, bt, btm, bt_q, btm_q


def _da_call(qkv, prev_out, cst, bt, btm, lam_p, g_col, *, seq, meta_queries, lambda_init):
    b, tp, _ = qkv.shape
    nkb = seq // ATT_BLK
    bq = META_BLK if meta_queries else ATT_BLK
    nq = 1 if meta_queries else nkb
    q_row0 = seq // bq if meta_queries else 0
    n_t, n_tm = bt.shape[1], btm.shape[1]
    in_specs = [
        pl.BlockSpec(memory_space=pltpu.SMEM),
        pl.BlockSpec((1, bq, LANES), lambda i, h, j: (i, q_row0 + j, h)),
        pl.BlockSpec((1, tp, LANES), lambda i, h, j: (i, 0, DA_HEADS + h)),
        pl.BlockSpec((1, tp, LANES), lambda i, h, j: (i, 0, 2 * DA_HEADS + h)),
        pl.BlockSpec((1, n_t, ATT_BLK, bq), lambda i, h, j: (h, 0, 0, 0)),
        pl.BlockSpec((1, n_tm, META_BLK, bq), lambda i, h, j: (h, 0, 0, 0)),
        pl.BlockSpec((4, DA_HEAD_DIM), lambda i, h, j: (0, 0)),
        pl.BlockSpec((LANES, 1), lambda i, h, j: (0, 0)),
    ]
    args = [cst, qkv, qkv, qkv, bt, btm, lam_p, g_col]
    aliases = {}
    if prev_out is not None:
        in_specs.append(pl.BlockSpec(memory_space=pl.ANY))
        args.append(prev_out)
        aliases = {len(args) - 1: 0}
    kern = functools.partial(_da_kernel, bq=bq, nkb=nkb, n_near=(n_t + 1) // 2, seq=seq,
                             meta_queries=meta_queries, lambda_init=lambda_init)
    if prev_out is not None:
        body = kern
        kern = lambda *refs: body(*refs[:8], *refs[9:])
    return pl.pallas_call(
        kern,
        out_shape=jax.ShapeDtypeStruct((b, tp, D_MODEL), BF16),
        grid=(b, DA_HEADS, nq),
        in_specs=in_specs,
        out_specs=pl.BlockSpec((1, bq, LANES), lambda i, h, j: (i, q_row0 + j, h)),
        scratch_shapes=[pltpu.VMEM((nkb, LANES + ONES_ROWS, ATT_BLK), BF16),
                        pltpu.VMEM((LANES + ONES_ROWS, META_BLK), BF16), pltpu.VMEM((2 * bq, LANES), BF16)]
        + [pltpu.VMEM(shape, dt)
           for shape, dt in ([((1, COL_CHUNK), F32), ((LANES + ONES_ROWS, COL_CHUNK), F32)]
                             + [((ATT_BLK, COL_CHUNK), F32)] * LOGIT_BUFS + [((1, COL_CHUNK), F32)] * LOGIT_BUFS)
           for _ in range(2 * bq // COL_CHUNK)],
        input_output_aliases=aliases,
        compiler_params=_params(("parallel", "parallel", "arbitrary")),
        name="da_meta_queries" if meta_queries else "da_attention",
    )(*args)


def kernel(x, meta_tokens, norm_g, ffn_w_gate, ffn_w_up, ffn_w_down, na_w_qkv, na_b_qkv, na_w_o, na_b_o, na_rpb, na_meta_bias, da_w_qkv, da_w_o, da_lambda, da_subln_g, t5_rel_bias):
    b, seq, d = x.shape
    depth = norm_g.shape[0]
    assert d == D_MODEL and seq % ATT_BLK == 0 and seq % GRID_W == 0
    tp = seq + META_BLK
    assert (b * tp) % ROW_TILE == 0
    meta = jnp.broadcast_to(meta_tokens[None].astype(x.dtype), (b, N_META, d))
    pad = jnp.zeros((b, META_BLK - N_META, d), x.dtype)
    h = jnp.concatenate([x, meta, pad], axis=1).reshape(b * tp, d)

    vec = lambda v: v.reshape(1, -1).astype(F32)
    zero_qkv_bias = jnp.zeros((1, 3 * d), F32)
    zero_o_bias = jnp.zeros((1, d), F32)
    da_cst = _da_tables(t5_rel_bias, seq)

    for i in range(depth):
        g = norm_g[i]
        j = i // N_MIXERS
        h = _ffn(h, vec(g[0]), vec(g[1]), ffn_w_gate[i, 0].astype(BF16), ffn_w_up[i, 0].astype(BF16),
                 ffn_w_down[i, 0].astype(BF16))
        if i % N_MIXERS == 0:
            qkv = _qkv(h, vec(g[2]), na_w_qkv[j].astype(BF16), vec(na_b_qkv[j]), (D_MODEL // NA_HEADS) ** -0.5)
            nab, mb = _na_tables(na_rpb[j], na_meta_bias[j], seq // GRID_W)
            a = _na_attention(qkv.reshape(b, tp, 3 * d), nab, mb, seq)
            h = _oproj(a.reshape(b * tp, d), h, na_w_o[j].astype(BF16), vec(na_b_o[j]), vec(g[3]))
        else:
            lambda_init = 0.8 - 0.6 * math.exp(-0.3 * i)
            qkv = _qkv(h, vec(g[2]), da_w_qkv[j].astype(BF16), zero_qkv_bias, DA_HEAD_DIM ** -0.5 * LOG2E)
            qkv = qkv.reshape(b, tp, 3 * d)
            cst, bt, btm, bt_q, btm_q = da_cst
            g_col = da_subln_g[j].reshape(LANES, 1).astype(F32)
            lam_p = da_lambda[j].astype(F32)
            a = _da_call(qkv, None, cst, bt, btm, lam_p, g_col, seq=seq, meta_queries=False,
                         lambda_init=lambda_init)
            a = _da_call(qkv, a, cst, bt_q, btm_q, lam_p, g_col, seq=seq, meta_queries=True,
                         lambda_init=lambda_init)
            h = _oproj(a.reshape(b * tp, d), h, da_w_o[j].astype(BF16), zero_o_bias, vec(g[3]))
        h = _ffn(h, vec(g[4]), vec(g[5]), ffn_w_gate[i, 1].astype(BF16), ffn_w_up[i, 1].astype(BF16),
                 ffn_w_down[i, 1].astype(BF16))
    return h.reshape(b, tp, d)[:, :seq]
```

```python
import functools
import math

import numpy as np
import jax
import jax.numpy as jnp
from jax import lax
from jax.experimental import pallas as pl
from jax.experimental.pallas import tpu as pltpu

F32 = jnp.float32
BF16 = jnp.bfloat16

D_MODEL = 1024
N_META = 16
GRID_W = 64
NA_KH = 8
NA_KW = 16
NA_HEADS = 16
DA_HEADS = 8
DA_HEAD_DIM = 64
T5_BUCKETS = 32
T5_MAX_DIST = 128
D_FF = 2816
FFN_RES = 0.5
RMS_EPS = 1e-6
N_MIXERS = 2

LANES = 128
META_BLK = 128
ROW_TILE = 512
ATT_BLK = 512
COL_CHUNK = 256
ROW_PIECE = 64
NA_ROWS_PER_STEP = 8
MXU_DEPTH = 256
LOGIT_BUFS = 3
ONES_ROWS = 16
EAGER_HALF_RANGE = 45.0
NORM_SLACK = 1.02
EAGER_V_MAX = 2.0 ** 20
LOG2E = math.log2(math.e)
NEG = -1e30
VMEM_LIMIT = 56 * 1024 * 1024


def _rms(x, g):
    return x * lax.rsqrt(jnp.mean(x * x, axis=-1, keepdims=True) + RMS_EPS) * g


def _params(sem):
    return pltpu.CompilerParams(dimension_semantics=sem, vmem_limit_bytes=VMEM_LIMIT)


def _resident(shape):
    return pl.BlockSpec(shape, lambda *_: (0,) * len(shape), pipeline_mode=pl.Buffered(1))


def _ffn_kernel(h_ref, gin_ref, gout_ref, wg_ref, wu_ref, wd_ref, o_ref, *, fc):
    x = h_ref[...]
    xn = _rms(x, gin_ref[...]).astype(BF16)
    acc = None
    for c in range(D_FF // fc):
        sl = slice(c * fc, (c + 1) * fc)
        g = jnp.dot(xn, wg_ref[:, sl], preferred_element_type=F32)
        u = jnp.dot(xn, wu_ref[:, sl], preferred_element_type=F32)
        hid = (g * jax.nn.sigmoid(g) * u).astype(BF16)
        d = jnp.dot(hid, wd_ref[sl, :], preferred_element_type=F32)
        acc = d if acc is None else acc + d
    o_ref[...] = x + FFN_RES * _rms(acc, gout_ref[...])


def _ffn(h, g_in, g_out, wg, wu, wd):
    mt = h.shape[0]
    row = pl.BlockSpec((ROW_TILE, D_MODEL), lambda i: (i, 0))
    return pl.pallas_call(
        functools.partial(_ffn_kernel, fc=1408),
        out_shape=jax.ShapeDtypeStruct((mt, D_MODEL), F32),
        grid=(mt // ROW_TILE,),
        in_specs=[row, _resident((1, D_MODEL)), _resident((1, D_MODEL)),
                  _resident((D_MODEL, D_FF)), _resident((D_MODEL, D_FF)), _resident((D_FF, D_MODEL))],
        out_specs=row,
        compiler_params=_params(("parallel",)),
        name="ffn",
    )(h, g_in, g_out, wg, wu, wd)


def _qkv_kernel(h_ref, g_ref, w_ref, b_ref, *rest, q_scale, with_stats):
    if with_stats:
        ind_ref, o_ref, st_ref = rest
    else:
        (o_ref,) = rest
    xn = _rms(h_ref[...], g_ref[...]).astype(BF16)
    stats = []
    for j in range(3):
        sl = slice(j * D_MODEL, (j + 1) * D_MODEL)
        y = jnp.dot(xn, w_ref[:, sl], preferred_element_type=F32) + b_ref[:, sl]
        if j == 0:
            y = y * q_scale
        o_ref[:, sl] = y.astype(BF16)
        if with_stats and j < 2:
            norm2 = jnp.dot((y * y).astype(BF16), ind_ref[...], preferred_element_type=F32)
            stats.append(jnp.max(norm2, axis=0, keepdims=True))
        elif with_stats:
            a = jnp.abs(y)
            fold = a[:, 0:LANES]
            for t in range(1, D_MODEL // LANES):
                fold = jnp.maximum(fold, a[:, t * LANES:(t + 1) * LANES])
            stats.append(jnp.max(fold, axis=0, keepdims=True))
    if with_stats:
        st_ref[0] = jnp.concatenate(stats + [jnp.zeros((8 - len(stats), LANES), F32)], axis=0)


def _qkv(h, g, w, b, q_scale, with_stats=False):
    mt = h.shape[0]
    n_tile = mt // ROW_TILE
    in_specs = [pl.BlockSpec((ROW_TILE, D_MODEL), lambda i: (i, 0)), _resident((1, D_MODEL)),
                _resident((D_MODEL, 3 * D_MODEL)), _resident((1, 3 * D_MODEL))]
    out_shape = [jax.ShapeDtypeStruct((mt, 3 * D_MODEL), BF16)]
    out_specs = [pl.BlockSpec((ROW_TILE, 3 * D_MODEL), lambda i: (i, 0))]
    args = [h, g, w, b]
    if with_stats:
        ind = (np.arange(D_MODEL)[:, None] // DA_HEAD_DIM == np.arange(LANES)[None, :])
        args.append(jnp.asarray(ind, BF16))
        in_specs.append(_resident((D_MODEL, LANES)))
        out_shape.append(jax.ShapeDtypeStruct((n_tile, 8, LANES), F32))
        out_specs.append(pl.BlockSpec((1, 8, LANES), lambda i: (i, 0, 0)))
    out = pl.pallas_call(
        functools.partial(_qkv_kernel, q_scale=q_scale, with_stats=with_stats),
        out_shape=out_shape,
        grid=(n_tile,),
        in_specs=in_specs,
        out_specs=out_specs,
        compiler_params=_params(("parallel",)),
        name="qkv_proj",
    )(*args)
    return out if with_stats else out[0]


def _oproj_kernel(a_ref, h_ref, w_ref, b_ref, g_ref, o_ref):
    m = jnp.dot(a_ref[...], w_ref[...], preferred_element_type=F32) + b_ref[...]
    o_ref[...] = h_ref[...] + _rms(m, g_ref[...])


def _oproj(a, h, w, b, g):
    mt = h.shape[0]
    row = pl.BlockSpec((ROW_TILE, D_MODEL), lambda i: (i, 0))
    return pl.pallas_call(
        _oproj_kernel,
        out_shape=jax.ShapeDtypeStruct((mt, D_MODEL), F32),
        grid=(mt // ROW_TILE,),
        in_specs=[row, row, _resident((D_MODEL, D_MODEL)), _resident((1, D_MODEL)), _resident((1, D_MODEL))],
        out_specs=row,
        compiler_params=_params(("parallel",)),
        name="out_proj",
    )(a, h, w, b, g)


def _na_kernel(q_ref, k_ref, v_ref, nab_ref, mb_ref, o_ref, *, rows, seq):
    kh = NA_KH
    lane = lax.broadcasted_iota(jnp.int32, (GRID_W, LANES), 1)
    head0 = lane < (LANES // 2)
    k_meta = k_ref[0, seq:seq + META_BLK, :]
    v_meta = v_ref[0, seq:seq + META_BLK, :]
    mb = jnp.concatenate([jnp.broadcast_to(mb_ref[0, 0], (GRID_W, LANES)),
                          jnp.broadcast_to(mb_ref[0, 1], (GRID_W, LANES))], axis=0)
    nt = (((1,), (1,)), ((), ()))

    def split_heads(q):
        zero = jnp.zeros_like(q)
        return jnp.concatenate([jnp.where(head0[:q.shape[0]], q, zero),
                                jnp.where(head0[:q.shape[0]], zero, q)], axis=0)

    def row_fn(r):
        rs = jnp.clip(r - kh // 2, 0, rows - kh)
        dy0 = rs - r + (NA_KH - 1)
        q2 = split_heads(q_ref[0, pl.ds(pl.multiple_of(r * GRID_W, GRID_W), GRID_W), :])
        kbase = pl.multiple_of(rs * GRID_W, GRID_W)
        k_win = k_ref[0, pl.ds(kbase, kh * GRID_W), :]
        v_win = v_ref[0, pl.ds(kbase, kh * GRID_W), :]
        bias = jnp.concatenate([nab_ref[0, 0, dy0], nab_ref[0, 1, dy0]], axis=0)
        s = lax.dot_general(q2, k_win, nt, preferred_element_type=F32) + bias
        sm = lax.dot_general(q2, k_meta, nt, preferred_element_type=F32) + mb
        m = jnp.maximum(jnp.max(s, axis=-1, keepdims=True), jnp.max(sm, axis=-1, keepdims=True))
        p = jnp.exp(s - m)
        pm = jnp.exp(sm - m)
        l = jnp.sum(p, axis=-1, keepdims=True) + jnp.sum(pm, axis=-1, keepdims=True)
        o2 = (jnp.dot(p.astype(BF16), v_win, preferred_element_type=F32)
              + jnp.dot(pm.astype(BF16), v_meta, preferred_element_type=F32)) / l
        o = jnp.where(head0, o2[:GRID_W], o2[GRID_W:])
        o_ref[0, pl.ds(pl.multiple_of(r * GRID_W, GRID_W), GRID_W), :] = o.astype(BF16)

    def row_group(i, carry):
        for u in range(NA_ROWS_PER_STEP):
            row_fn(i * NA_ROWS_PER_STEP + u)
        return carry

    lax.fori_loop(0, rows // NA_ROWS_PER_STEP, row_group, 0)

    qm = q_ref[0, seq:seq + META_BLK, :]
    lane_m = lax.broadcasted_iota(jnp.int32, (META_BLK, LANES), 1) < (LANES // 2)
    zero = jnp.zeros_like(qm)
    q2 = jnp.concatenate([jnp.where(lane_m, qm, zero), jnp.where(lane_m, zero, qm)], axis=0)
    mbm = jnp.concatenate([jnp.broadcast_to(mb_ref[0, 0], (META_BLK, LANES)),
                           jnp.broadcast_to(mb_ref[0, 1], (META_BLK, LANES))], axis=0)
    sm = lax.dot_general(q2, k_meta, nt, preferred_element_type=F32) + mbm
    pm = jnp.exp(sm - jnp.max(sm, axis=-1, keepdims=True))
    o2 = jnp.dot(pm.astype(BF16), v_meta, preferred_element_type=F32) / jnp.sum(pm, axis=-1, keepdims=True)
    o_ref[0, seq:seq + META_BLK, :] = jnp.where(lane_m, o2[:META_BLK], o2[META_BLK:]).astype(BF16)


def _toeplitz(w, n, m):
    period = w.shape[-1]
    width = period - 1
    assert m <= width
    reps = -(-(n * width) // period)
    flat = jnp.tile(w, (1,) * (w.ndim - 1) + (reps,))[..., :n * width]
    return flat.reshape(w.shape[:-1] + (n, width))[..., :m]


def _na_tables(rpb, meta_bias, rows):
    kh = NA_KH
    qc = np.arange(GRID_W)[:, None]
    kc = np.arange(GRID_W)[None, :]
    cs = np.clip(qc - NA_KW // 2, 0, GRID_W - NA_KW)
    valid = (kc >= cs) & (kc < cs + NA_KW)
    period = 2 * GRID_W
    off = np.arange(period)
    off = np.where(off < GRID_W, off, off - period)
    w = rpb[:, :, np.clip(off + NA_KW - 1, 0, 2 * NA_KW - 2)]
    t = jnp.where(valid, _toeplitz(w, GRID_W, GRID_W), NEG)
    t = jnp.stack([t[:, dy0:dy0 + kh].transpose(0, 2, 1, 3) for dy0 in range(kh)], axis=1)
    t = t.reshape(NA_HEADS // 2, 2, kh, GRID_W, kh * GRID_W)
    mb = jnp.full((NA_HEADS, LANES), NEG, F32).at[:, :N_META].set(meta_bias)
    return t.astype(F32), mb.reshape(NA_HEADS // 2, 2, 1, LANES)


def _na_attention(qkv, nab, mb, seq):
    b, tp, _ = qkv.shape
    rows = seq // GRID_W
    n_slab = D_MODEL // LANES
    slab = lambda off: pl.BlockSpec((1, tp, LANES), lambda i, s: (i, 0, off + s))
    return pl.pallas_call(
        functools.partial(_na_kernel, rows=rows, seq=seq),
        out_shape=jax.ShapeDtypeStruct((b, tp, D_MODEL), BF16),
        grid=(b, n_slab),
        in_specs=[slab(0), slab(n_slab), slab(2 * n_slab),
                  pl.BlockSpec((1, 2, NA_KH, GRID_W, NA_KH * GRID_W), lambda i, s: (s, 0, 0, 0, 0)),
                  pl.BlockSpec((1, 2, 1, LANES), lambda i, s: (s, 0, 0, 0))],
        out_specs=slab(0),
        compiler_params=_params(("parallel", "parallel")),
        name="na_attention",
    )(qkv, qkv, qkv, nab, mb)


def _da_kernel(cst_ref, q_ref, k_ref, v_ref, bt_ref, btm_ref, lamp_ref, g_ref, o_ref,
               vt_scr, vtm_scr, q2_scr, *chunk_scr, bq, nkb, n_near, seq, meta_queries, lambda_init,
               eager_probs):
    bk = ATT_BLK
    n_chunk = 2 * bq // COL_CHUNK
    groups = [chunk_scr[i * n_chunk:(i + 1) * n_chunk] for i in range(3 + 2 * LOGIT_BUFS)]
    m_scr, mlast_scr, acc_scr = groups[:3]
    blk_scr, row_scr = groups[3:3 + LOGIT_BUFS], groups[3 + LOGIT_BUFS:]
    hd = pl.program_id(1)
    qi = pl.program_id(2)
    nt = (((1,), (1,)), ((), ()))

    @pl.when(qi == 0)
    def _():
        for kb in range(nkb):
            vt_scr[kb, 0:LANES, :] = v_ref[0, kb * bk:(kb + 1) * bk, :].astype(F32).T.astype(BF16)
            vt_scr[kb, LANES:, :] = jnp.ones((ONES_ROWS, bk), BF16)
        vtm_scr[0:LANES, :] = v_ref[0, seq:seq + META_BLK, :].astype(F32).T.astype(BF16)
        vtm_scr[LANES:, :] = jnp.ones((ONES_ROWS, META_BLK), BF16)

    q = q_ref[0]
    first = lax.broadcasted_iota(jnp.int32, q.shape, 1) < DA_HEAD_DIM
    zero = jnp.zeros_like(q)
    q2_scr[0:bq, :] = jnp.where(first, q, zero)
    q2_scr[bq:2 * bq, :] = jnp.where(first, zero, q)
    for cc in range(n_chunk):
        m_scr[cc][...] = jnp.full(m_scr[cc].shape, NEG, F32)
        mlast_scr[cc][...] = jnp.full(mlast_scr[cc].shape, NEG, F32)
        acc_scr[cc][...] = jnp.zeros(acc_scr[cc].shape, F32)
    c_neg = cst_ref[hd, 0]
    c_pos = cst_ref[hd, 1]

    def logits(k_blk, buf, c, bias=None, first=False):
        nk = k_blk.shape[0]
        for cc in range(n_chunk):
            s = lax.dot_general(k_blk, q2_scr[cc * COL_CHUNK:(cc + 1) * COL_CHUNK, :], nt,
                                preferred_element_type=F32)
            if bias is not None:
                if bq >= COL_CHUNK:
                    b0 = (cc * COL_CHUNK) % bq
                    s = s + bias[:, b0:b0 + COL_CHUNK]
                else:
                    s = s + jnp.concatenate([bias[...]] * (COL_CHUNK // bq), axis=1)
            s_max = jnp.max(s, axis=0, keepdims=True) + c
            if not eager_probs:
                blk_scr[buf][cc][0:nk, :] = s
                row_scr[buf][cc][...] = s_max
                continue
            m_run = m_scr[cc][...]
            scale = s_max if first else m_run
            blk_scr[buf][cc][0:nk, :] = jnp.exp2(s - (scale - c)).astype(BF16)
            row_scr[buf][cc][...] = jnp.exp2(mlast_scr[cc][...] - scale)
            mlast_scr[cc][...] = scale
            m_scr[cc][...] = s_max if first else jnp.maximum(m_run, s_max)

    def softmax_pv(vt, c, buf):
        nk = vt.shape[1]
        kd = min(nk, MXU_DEPTH)
        for cc in range(n_chunk):
            if eager_probs:
                acc = acc_scr[cc][...] * row_scr[buf][cc][...]
                for k0 in range(0, nk, kd):
                    acc = acc + jnp.dot(vt[:, k0:k0 + kd], blk_scr[buf][cc][k0:k0 + kd, :],
                                        preferred_element_type=F32)
            else:
                m_old = m_scr[cc][...]
                m_new = jnp.maximum(m_old, row_scr[buf][cc][...])
                shift = m_new - c
                acc = acc_scr[cc][...] * jnp.exp2(m_old - m_new)
                for k0 in range(0, nk, kd):
                    p = jnp.concatenate(
                        [jnp.exp2(blk_scr[buf][cc][r0:r0 + ROW_PIECE, :] - shift).astype(BF16)
                         for r0 in range(k0, k0 + kd, ROW_PIECE)], axis=0)
                    acc = acc + jnp.dot(vt[:, k0:k0 + kd], p, preferred_element_type=F32)
                m_scr[cc][...] = m_new
            acc_scr[cc][...] = acc

    n_far = nkb - n_near
    lo = 0 if meta_queries else jnp.clip(qi - n_near // 2, 0, nkb - n_near)
    tm = 0 if meta_queries else jnp.minimum(qi, 1)
    key_blk = lambda kb: k_ref[0, pl.ds(pl.multiple_of(kb * bk, bk), bk), :]
    far_kb = lambda j: jnp.where(j < lo, j, j + n_near)

    def issue_logits(t, buf, first=False):
        if isinstance(t, int) and t == nkb:
            logits(k_ref[0, seq:seq + META_BLK, :], buf, 0.0, btm_ref.at[0, tm])
        elif isinstance(t, int) and t >= n_far:
            kb = lo + (t - n_far)
            logits(key_blk(kb), buf, 0.0, bt_ref.at[0, 0 if meta_queries else kb - qi + n_near - 1])
        else:
            logits(key_blk(far_kb(t)), buf, jnp.where(t < lo, c_neg, c_pos), first=first)

    def finish(t, buf):
        if isinstance(t, int) and t == nkb:
            softmax_pv(vtm_scr[...], 0.0, buf)
        elif isinstance(t, int) and t >= n_far:
            softmax_pv(vt_scr[lo + (t - n_far)], 0.0, buf)
        else:
            softmax_pv(vt_scr[far_kb(t)], jnp.where(t < lo, c_neg, c_pos), buf)

    issue_logits(0, 0, first=True)
    n_group = (n_far - 1) // LOGIT_BUFS

    def group_body(i, carry):
        for u in range(LOGIT_BUFS):
            issue_logits(LOGIT_BUFS * i + u + 1, (u + 1) % LOGIT_BUFS)
            finish(LOGIT_BUFS * i + u, u)
        return carry

    lax.fori_loop(0, n_group, group_body, 0)
    for t in range(LOGIT_BUFS * n_group, nkb + 1):
        if t < nkb:
            issue_logits(t + 1, (t + 1) % LOGIT_BUFS)
        finish(t, t % LOGIT_BUFS)

    lp = lamp_ref[...]
    lam = (jnp.exp(jnp.sum(lp[0:1] * lp[1:2], axis=1, keepdims=True))
           - jnp.exp(jnp.sum(lp[2:3] * lp[3:4], axis=1, keepdims=True)) + lambda_init)
    on = jnp.concatenate([acc_scr[cc][0:LANES, :] / acc_scr[cc][LANES:LANES + 1, :] for cc in range(n_chunk)],
                         axis=1)
    ot = on[:, :bq] - lam * on[:, bq:]
    y = ot * lax.rsqrt(jnp.mean(ot * ot, axis=0, keepdims=True) + RMS_EPS) * g_ref[...]
    o_ref[0] = (y * (1.0 - lambda_init)).T.astype(BF16)


def _t5_bucket(rel):
    nb = T5_BUCKETS // 2
    max_exact = nb // 2
    ret = jnp.where(rel > 0, nb, 0)
    n = jnp.abs(rel)
    nf = jnp.maximum(n, 1).astype(jnp.float32)
    large = max_exact + (jnp.log(nf / max_exact) / math.log(T5_MAX_DIST / max_exact)
                         * (nb - max_exact)).astype(jnp.int32)
    large = jnp.minimum(large, nb - 1)
    return ret + jnp.where(n < max_exact, n, large)


def _da_tables(rel_table, seq):
    blk = ATT_BLK
    n_head = rel_table.shape[1]
    reach = 2 * blk - 1
    rel_table = rel_table.astype(F32) * LOG2E
    line = rel_table[_t5_bucket(jnp.arange(-reach, reach + 1, dtype=jnp.int32))].T
    val = lambda rel: line[:, np.clip(np.asarray(rel) + reach, 0, 2 * reach)]
    cst = jnp.stack([rel_table[T5_BUCKETS // 2 - 1], rel_table[T5_BUCKETS - 1]], axis=1)
    qq = np.arange(blk)[None, :]
    period = 2 * blk
    off = np.arange(period)
    off = np.where(off < blk, off, off - period)
    flat = lambda col: jnp.broadcast_to(cst[:, col, None, None], (n_head, blk, blk))
    bt = jnp.stack([flat(0)] + [_toeplitz(val(d * blk - off), blk, blk) for d in (-1, 0, 1)] + [flat(1)],
                   axis=1)
    jm = np.arange(N_META)[:, None]
    first = val(jm - (qq + N_META))
    later = jnp.broadcast_to(cst[:, 0, None, None], first.shape)
    pad_keys = lambda cols, n: jnp.full((n_head, n, META_BLK - N_META, cols), NEG, F32)
    btm = jnp.concatenate([jnp.stack([first, later], axis=1), pad_keys(blk, 2)], axis=2)
    pad_cols = lambda t: jnp.concatenate(
        [t, jnp.broadcast_to(t[..., -1:], t.shape[:-1] + (META_BLK - N_META,))], axis=-1)
    im = np.arange(N_META)[None, :]
    bt_q = pad_cols(val(np.arange(blk)[:, None] + N_META - im))[:, None]
    btm_q = jnp.concatenate([pad_cols(val(jm - im))[:, None], pad_keys(META_BLK, 1)], axis=2)
    return cst, bt, btm, bt_q, btm_q


def _da_logits_bounded(stats, rel_table):
    st = jnp.max(stats, axis=0)
    n_half = 2 * DA_HEADS
    dot_bound = jnp.sqrt(st[0, :n_half] * st[1, :n_half]).reshape(DA_HEADS, 2) * NORM_SLACK
    bias_bound = jnp.max(jnp.abs(rel_table.astype(F32)), axis=0) * LOG2E
    logit_bound = jnp.max(jnp.max(dot_bound, axis=1) + bias_bound)
    return (logit_bound <= EAGER_HALF_RANGE) & (jnp.max(st[2]) <= EAGER_V_MAX)


def _da_call(qkv, prev_out, cst, bt, btm, lam_p, g_col, *, seq, meta_queries, lambda_init, eager_probs):
    b, tp, _ = qkv.shape
    nkb = seq // ATT_BLK
    bq = META_BLK if meta_queries else ATT_BLK
    nq = 1 if meta_queries else nkb
    q_row0 = seq // bq if meta_queries else 0
    n_t, n_tm = bt.shape[1], btm.shape[1]
    in_specs = [
        pl.BlockSpec(memory_space=pltpu.SMEM),
        pl.BlockSpec((1, bq, LANES), lambda i, h, j: (i, q_row0 + j, h)),
        pl.BlockSpec((1, tp, LANES), lambda i, h, j: (i, 0, DA_HEADS + h)),
        pl.BlockSpec((1, tp, LANES), lambda i, h, j: (i, 0, 2 * DA_HEADS + h)),
        pl.BlockSpec((1, n_t, ATT_BLK, bq), lambda i, h, j: (h, 0, 0, 0)),
        pl.BlockSpec((1, n_tm, META_BLK, bq), lambda i, h, j: (h, 0, 0, 0)),
        pl.BlockSpec((4, DA_HEAD_DIM), lambda i, h, j: (0, 0)),
        pl.BlockSpec((LANES, 1), lambda i, h, j: (0, 0)),
    ]
    args = [cst, qkv, qkv, qkv, bt, btm, lam_p, g_col]
    aliases = {}
    if prev_out is not None:
        in_specs.append(pl.BlockSpec(memory_space=pl.ANY))
        args.append(prev_out)
        aliases = {len(args) - 1: 0}
    kern = functools.partial(_da_kernel, bq=bq, nkb=nkb, n_near=(n_t + 1) // 2, seq=seq,
                             meta_queries=meta_queries, lambda_init=lambda_init, eager_probs=eager_probs)
    if prev_out is not None:
        body = kern
        kern = lambda *refs: body(*refs[:8], *refs[9:])
    return pl.pallas_call(
        kern,
        out_shape=jax.ShapeDtypeStruct((b, tp, D_MODEL), BF16),
        grid=(b, DA_HEADS, nq),
        in_specs=in_specs,
        out_specs=pl.BlockSpec((1, bq, LANES), lambda i, h, j: (i, q_row0 + j, h)),
        scratch_shapes=[pltpu.VMEM((nkb, LANES + ONES_ROWS, ATT_BLK), BF16),
                        pltpu.VMEM((LANES + ONES_ROWS, META_BLK), BF16), pltpu.VMEM((2 * bq, LANES), BF16)]
        + [pltpu.VMEM(shape, dt)
           for shape, dt in ([((1, COL_CHUNK), F32), ((1, COL_CHUNK), F32), ((LANES + ONES_ROWS, COL_CHUNK), F32)]
                             + [((ATT_BLK, COL_CHUNK), BF16 if eager_probs else F32)] * LOGIT_BUFS
                             + [((1, COL_CHUNK), F32)] * LOGIT_BUFS)
           for _ in range(2 * bq // COL_CHUNK)],
        input_output_aliases=aliases,
        compiler_params=_params(("parallel", "parallel", "arbitrary")),
        name="da_meta_queries" if meta_queries else "da_attention",
    )(*args)


def kernel(x, meta_tokens, norm_g, ffn_w_gate, ffn_w_up, ffn_w_down, na_w_qkv, na_b_qkv, na_w_o, na_b_o, na_rpb, na_meta_bias, da_w_qkv, da_w_o, da_lambda, da_subln_g, t5_rel_bias):
    b, seq, d = x.shape
    depth = norm_g.shape[0]
    assert d == D_MODEL and seq % ATT_BLK == 0 and seq % GRID_W == 0
    tp = seq + META_BLK
    assert (b * tp) % ROW_TILE == 0
    meta = jnp.broadcast_to(meta_tokens[None].astype(x.dtype), (b, N_META, d))
    pad = jnp.zeros((b, META_BLK - N_META, d), x.dtype)
    h = jnp.concatenate([x, meta, pad], axis=1).reshape(b * tp, d)

    vec = lambda v: v.reshape(1, -1).astype(F32)
    zero_qkv_bias = jnp.zeros((1, 3 * d), F32)
    zero_o_bias = jnp.zeros((1, d), F32)
    da_cst = _da_tables(t5_rel_bias, seq)

    for i in range(depth):
        g = norm_g[i]
        j = i // N_MIXERS
        h = _ffn(h, vec(g[0]), vec(g[1]), ffn_w_gate[i, 0].astype(BF16), ffn_w_up[i, 0].astype(BF16),
                 ffn_w_down[i, 0].astype(BF16))
        if i % N_MIXERS == 0:
            qkv = _qkv(h, vec(g[2]), na_w_qkv[j].astype(BF16), vec(na_b_qkv[j]), (D_MODEL // NA_HEADS) ** -0.5)
            nab, mb = _na_tables(na_rpb[j], na_meta_bias[j], seq // GRID_W)
            a = _na_attention(qkv.reshape(b, tp, 3 * d), nab, mb, seq)
            h = _oproj(a.reshape(b * tp, d), h, na_w_o[j].astype(BF16), vec(na_b_o[j]), vec(g[3]))
        else:
            lambda_init = 0.8 - 0.6 * math.exp(-0.3 * i)
            qkv, qkv_stats = _qkv(h, vec(g[2]), da_w_qkv[j].astype(BF16), zero_qkv_bias,
                                  DA_HEAD_DIM ** -0.5 * LOG2E, with_stats=True)
            qkv = qkv.reshape(b, tp, 3 * d)
            cst, bt, btm, bt_q, btm_q = da_cst
            g_col = da_subln_g[j].reshape(LANES, 1).astype(F32)
            lam_p = da_lambda[j].astype(F32)
            def attend(eager_probs):
                a = _da_call(qkv, None, cst, bt, btm, lam_p, g_col, seq=seq, meta_queries=False,
                             lambda_init=lambda_init, eager_probs=eager_probs)
                return _da_call(qkv, a, cst, bt_q, btm_q, lam_p, g_col, seq=seq, meta_queries=True,
                                lambda_init=lambda_init, eager_probs=eager_probs)

            a = lax.cond(_da_logits_bounded(qkv_stats, t5_rel_bias), functools.partial(attend, True),
                         functools.partial(attend, False))
            h = _oproj(a.reshape(b * tp, d), h, da_w_o[j].astype(BF16), zero_o_bias, vec(g[3]))
        h = _ffn(h, vec(g[4]), vec(g[5]), ffn_w_gate[i, 1].astype(BF16), ffn_w_up[i, 1].astype(BF16),
                 ffn_w_down[i, 1].astype(BF16))
    return h.reshape(b, tp, d)[:, :seq]
```

```python
import functools
import math

import numpy as np
import jax
import jax.numpy as jnp
from jax import lax
from jax.experimental import pallas as pl
from jax.experimental.pallas import tpu as pltpu

F32 = jnp.float32
BF16 = jnp.bfloat16

D_MODEL = 1024
N_META = 16
GRID_W = 64
NA_KH = 8
NA_KW = 16
NA_HEADS = 16
DA_HEADS = 8
DA_HEAD_DIM = 64
T5_BUCKETS = 32
T5_MAX_DIST = 128
D_FF = 2816
FFN_RES = 0.5
RMS_EPS = 1e-6
N_MIXERS = 2

LANES = 128
META_BLK = 128
ROW_TILE = 512
FFN_HIDDEN_CHUNK = 1408
ATT_BLK = 512
COL_CHUNK = 256
ROW_PIECE = 64
NA_ROWS_PER_STEP = 8
MXU_DEPTH = 256
LOGIT_BUFS = 3
ONES_ROWS = 16
EAGER_HALF_RANGE = 45.0
NORM_SLACK = 1.02
EAGER_V_MAX = 2.0 ** 20
LOG2E = math.log2(math.e)
NEG = -1e30
VMEM_LIMIT = 56 * 1024 * 1024


def _rms(x, g):
    return x * lax.rsqrt(jnp.mean(x * x, axis=-1, keepdims=True) + RMS_EPS) * g


def _params(sem):
    return pltpu.CompilerParams(dimension_semantics=sem, vmem_limit_bytes=VMEM_LIMIT)


def _resident(shape):
    return pl.BlockSpec(shape, lambda *_: (0,) * len(shape), pipeline_mode=pl.Buffered(1))


def _swiglu_step(x, gin_ref, gout_ref, wg_ref, wu_ref, wd_ref):
    xn = _rms(x, gin_ref[...]).astype(BF16)
    acc = None
    for c in range(D_FF // FFN_HIDDEN_CHUNK):
        sl = slice(c * FFN_HIDDEN_CHUNK, (c + 1) * FFN_HIDDEN_CHUNK)
        g = jnp.dot(xn, wg_ref[:, sl], preferred_element_type=F32)
        u = jnp.dot(xn, wu_ref[:, sl], preferred_element_type=F32)
        hid = (g * jax.nn.sigmoid(g) * u).astype(BF16)
        d = jnp.dot(hid, wd_ref[sl, :], preferred_element_type=F32)
        acc = d if acc is None else acc + d
    return x + FFN_RES * _rms(acc, gout_ref[...])


def _ffn_kernel(h_ref, gin_ref, gout_ref, wg_ref, wu_ref, wd_ref, o_ref):
    o_ref[...] = _swiglu_step(h_ref[...], gin_ref, gout_ref, wg_ref, wu_ref, wd_ref)


def _ffn_weight_specs():
    return [_resident((1, D_MODEL)), _resident((1, D_MODEL)),
            _resident((D_MODEL, D_FF)), _resident((D_MODEL, D_FF)), _resident((D_FF, D_MODEL))]


def _ffn(h, g_in, g_out, wg, wu, wd):
    mt = h.shape[0]
    row = pl.BlockSpec((ROW_TILE, D_MODEL), lambda i: (i, 0))
    return pl.pallas_call(
        _ffn_kernel,
        out_shape=jax.ShapeDtypeStruct((mt, D_MODEL), F32),
        grid=(mt // ROW_TILE,),
        in_specs=[row] + _ffn_weight_specs(),
        out_specs=row,
        compiler_params=_params(("parallel",)),
        name="ffn",
    )(h, g_in, g_out, wg, wu, wd)


def _mix_ffn_kernel(a_ref, h_ref, wo_ref, bo_ref, gm_ref, gin_ref, gout_ref, wg_ref, wu_ref, wd_ref, o_ref):
    m = jnp.dot(a_ref[...], wo_ref[...], preferred_element_type=F32) + bo_ref[...]
    h1 = h_ref[...] + _rms(m, gm_ref[...])
    o_ref[...] = _swiglu_step(h1, gin_ref, gout_ref, wg_ref, wu_ref, wd_ref)


def _mix_ffn(a, h, wo, bo, g_mix, g_in, g_out, wg, wu, wd, out_rows=None):
    b, tp, _ = h.shape
    rows = tp if out_rows is None else out_rows
    tile = next(t for t in (5 * META_BLK, ROW_TILE, META_BLK) if rows % t == 0)
    blk = lambda width: pl.BlockSpec((None, tile, width), lambda i, j: (i, j, 0))
    return pl.pallas_call(
        _mix_ffn_kernel,
        out_shape=jax.ShapeDtypeStruct((b, rows, D_MODEL), F32),
        grid=(b, rows // tile),
        in_specs=[blk(D_MODEL), blk(D_MODEL), _resident((D_MODEL, D_MODEL)), _resident((1, D_MODEL)),
                  _resident((1, D_MODEL))] + _ffn_weight_specs(),
        out_specs=blk(D_MODEL),
        compiler_params=_params(("parallel", "parallel")),
        name="mix_ffn",
    )(a, h, wo, bo, g_mix, g_in, g_out, wg, wu, wd)


def _qkv_kernel(h_ref, g_ref, w_ref, b_ref, *rest, q_scale, with_stats):
    if with_stats:
        ind_ref, o_ref, st_ref = rest
    else:
        (o_ref,) = rest
    xn = _rms(h_ref[...], g_ref[...]).astype(BF16)
    stats = []
    for j in range(3):
        sl = slice(j * D_MODEL, (j + 1) * D_MODEL)
        y = jnp.dot(xn, w_ref[:, sl], preferred_element_type=F32) + b_ref[:, sl]
        if j == 0:
            y = y * q_scale
        o_ref[:, sl] = y.astype(BF16)
        if with_stats and j < 2:
            norm2 = jnp.dot((y * y).astype(BF16), ind_ref[...], preferred_element_type=F32)
            stats.append(jnp.max(norm2, axis=0, keepdims=True))
        elif with_stats:
            a = jnp.abs(y)
            fold = a[:, 0:LANES]
            for t in range(1, D_MODEL // LANES):
                fold = jnp.maximum(fold, a[:, t * LANES:(t + 1) * LANES])
            stats.append(jnp.max(fold, axis=0, keepdims=True))
    if with_stats:
        st_ref[0] = jnp.concatenate(stats + [jnp.zeros((8 - len(stats), LANES), F32)], axis=0)


def _qkv(h, g, w, b, q_scale, with_stats=False):
    mt = h.shape[0]
    n_tile = mt // ROW_TILE
    in_specs = [pl.BlockSpec((ROW_TILE, D_MODEL), lambda i: (i, 0)), _resident((1, D_MODEL)),
                _resident((D_MODEL, 3 * D_MODEL)), _resident((1, 3 * D_MODEL))]
    out_shape = [jax.ShapeDtypeStruct((mt, 3 * D_MODEL), BF16)]
    out_specs = [pl.BlockSpec((ROW_TILE, 3 * D_MODEL), lambda i: (i, 0))]
    args = [h, g, w, b]
    if with_stats:
        ind = (np.arange(D_MODEL)[:, None] // DA_HEAD_DIM == np.arange(LANES)[None, :])
        args.append(jnp.asarray(ind, BF16))
        in_specs.append(_resident((D_MODEL, LANES)))
        out_shape.append(jax.ShapeDtypeStruct((n_tile, 8, LANES), F32))
        out_specs.append(pl.BlockSpec((1, 8, LANES), lambda i: (i, 0, 0)))
    out = pl.pallas_call(
        functools.partial(_qkv_kernel, q_scale=q_scale, with_stats=with_stats),
        out_shape=out_shape,
        grid=(n_tile,),
        in_specs=in_specs,
        out_specs=out_specs,
        compiler_params=_params(("parallel",)),
        name="qkv_proj",
    )(*args)
    return out if with_stats else out[0]


def _na_kernel(q_ref, k_ref, v_ref, nab_ref, mb_ref, o_ref, *, rows, seq):
    kh = NA_KH
    lane = lax.broadcasted_iota(jnp.int32, (GRID_W, LANES), 1)
    head0 = lane < (LANES // 2)
    k_meta = k_ref[0, seq:seq + META_BLK, :]
    v_meta = v_ref[0, seq:seq + META_BLK, :]
    mb = jnp.concatenate([jnp.broadcast_to(mb_ref[0, 0], (GRID_W, LANES)),
                          jnp.broadcast_to(mb_ref[0, 1], (GRID_W, LANES))], axis=0)
    nt = (((1,), (1,)), ((), ()))

    def split_heads(q):
        zero = jnp.zeros_like(q)
        return jnp.concatenate([jnp.where(head0[:q.shape[0]], q, zero),
                                jnp.where(head0[:q.shape[0]], zero, q)], axis=0)

    def row_fn(r):
        rs = jnp.clip(r - kh // 2, 0, rows - kh)
        dy0 = rs - r + (NA_KH - 1)
        q2 = split_heads(q_ref[0, pl.ds(pl.multiple_of(r * GRID_W, GRID_W), GRID_W), :])
        kbase = pl.multiple_of(rs * GRID_W, GRID_W)
        k_win = k_ref[0, pl.ds(kbase, kh * GRID_W), :]
        v_win = v_ref[0, pl.ds(kbase, kh * GRID_W), :]
        bias = jnp.concatenate([nab_ref[0, 0, dy0], nab_ref[0, 1, dy0]], axis=0)
        s = lax.dot_general(q2, k_win, nt, preferred_element_type=F32) + bias
        sm = lax.dot_general(q2, k_meta, nt, preferred_element_type=F32) + mb
        m = jnp.maximum(jnp.max(s, axis=-1, keepdims=True), jnp.max(sm, axis=-1, keepdims=True))
        p = jnp.exp(s - m)
        pm = jnp.exp(sm - m)
        l = jnp.sum(p, axis=-1, keepdims=True) + jnp.sum(pm, axis=-1, keepdims=True)
        o2 = (jnp.dot(p.astype(BF16), v_win, preferred_element_type=F32)
              + jnp.dot(pm.astype(BF16), v_meta, preferred_element_type=F32)) / l
        o = jnp.where(head0, o2[:GRID_W], o2[GRID_W:])
        o_ref[0, pl.ds(pl.multiple_of(r * GRID_W, GRID_W), GRID_W), :] = o.astype(BF16)

    def row_group(i, carry):
        for u in range(NA_ROWS_PER_STEP):
            row_fn(i * NA_ROWS_PER_STEP + u)
        return carry

    lax.fori_loop(0, rows // NA_ROWS_PER_STEP, row_group, 0)

    qm = q_ref[0, seq:seq + META_BLK, :]
    lane_m = lax.broadcasted_iota(jnp.int32, (META_BLK, LANES), 1) < (LANES // 2)
    zero = jnp.zeros_like(qm)
    q2 = jnp.concatenate([jnp.where(lane_m, qm, zero), jnp.where(lane_m, zero, qm)], axis=0)
    mbm = jnp.concatenate([jnp.broadcast_to(mb_ref[0, 0], (META_BLK, LANES)),
                           jnp.broadcast_to(mb_ref[0, 1], (META_BLK, LANES))], axis=0)
    sm = lax.dot_general(q2, k_meta, nt, preferred_element_type=F32) + mbm
    pm = jnp.exp(sm - jnp.max(sm, axis=-1, keepdims=True))
    o2 = jnp.dot(pm.astype(BF16), v_meta, preferred_element_type=F32) / jnp.sum(pm, axis=-1, keepdims=True)
    o_ref[0, seq:seq + META_BLK, :] = jnp.where(lane_m, o2[:META_BLK], o2[META_BLK:]).astype(BF16)


def _toeplitz(w, n, m):
    period = w.shape[-1]
    width = period - 1
    assert m <= width
    reps = -(-(n * width) // period)
    flat = jnp.tile(w, (1,) * (w.ndim - 1) + (reps,))[..., :n * width]
    return flat.reshape(w.shape[:-1] + (n, width))[..., :m]


def _na_tables(rpb, meta_bias, rows):
    kh = NA_KH
    qc = np.arange(GRID_W)[:, None]
    kc = np.arange(GRID_W)[None, :]
    cs = np.clip(qc - NA_KW // 2, 0, GRID_W - NA_KW)
    valid = (kc >= cs) & (kc < cs + NA_KW)
    period = 2 * GRID_W
    off = np.arange(period)
    off = np.where(off < GRID_W, off, off - period)
    w = rpb[:, :, np.clip(off + NA_KW - 1, 0, 2 * NA_KW - 2)]
    t = jnp.where(valid, _toeplitz(w, GRID_W, GRID_W), NEG)
    t = jnp.stack([t[:, dy0:dy0 + kh].transpose(0, 2, 1, 3) for dy0 in range(kh)], axis=1)
    t = t.reshape(NA_HEADS // 2, 2, kh, GRID_W, kh * GRID_W)
    mb = jnp.full((NA_HEADS, LANES), NEG, F32).at[:, :N_META].set(meta_bias)
    return t.astype(F32), mb.reshape(NA_HEADS // 2, 2, 1, LANES)


def _na_attention(qkv, nab, mb, seq):
    b, tp, _ = qkv.shape
    rows = seq // GRID_W
    n_slab = D_MODEL // LANES
    slab = lambda off: pl.BlockSpec((1, tp, LANES), lambda i, s: (i, 0, off + s))
    return pl.pallas_call(
        functools.partial(_na_kernel, rows=rows, seq=seq),
        out_shape=jax.ShapeDtypeStruct((b, tp, D_MODEL), BF16),
        grid=(b, n_slab),
        in_specs=[slab(0), slab(n_slab), slab(2 * n_slab),
                  pl.BlockSpec((1, 2, NA_KH, GRID_W, NA_KH * GRID_W), lambda i, s: (s, 0, 0, 0, 0)),
                  pl.BlockSpec((1, 2, 1, LANES), lambda i, s: (s, 0, 0, 0))],
        out_specs=slab(0),
        compiler_params=_params(("parallel", "parallel")),
        name="na_attention",
    )(qkv, qkv, qkv, nab, mb)


def _da_kernel(cst_ref, q_ref, k_ref, v_ref, bt_ref, btm_ref, lamp_ref, g_ref, o_ref,
               vt_scr, vtm_scr, q2_scr, *chunk_scr, bq, nkb, n_near, seq, meta_queries, lambda_init,
               eager_probs):
    bk = ATT_BLK
    n_chunk = 2 * bq // COL_CHUNK
    groups = [chunk_scr[i * n_chunk:(i + 1) * n_chunk] for i in range(3 + 2 * LOGIT_BUFS)]
    m_scr, mlast_scr, acc_scr = groups[:3]
    blk_scr, row_scr = groups[3:3 + LOGIT_BUFS], groups[3 + LOGIT_BUFS:]
    hd = pl.program_id(1)
    qi = pl.program_id(2)
    nt = (((1,), (1,)), ((), ()))

    @pl.when(qi == 0)
    def _():
        for kb in range(nkb):
            vt_scr[kb, 0:LANES, :] = v_ref[0, kb * bk:(kb + 1) * bk, :].astype(F32).T.astype(BF16)
            vt_scr[kb, LANES:, :] = jnp.ones((ONES_ROWS, bk), BF16)
        vtm_scr[0:LANES, :] = v_ref[0, seq:seq + META_BLK, :].astype(F32).T.astype(BF16)
        vtm_scr[LANES:, :] = jnp.ones((ONES_ROWS, META_BLK), BF16)

    q = q_ref[0]
    first = lax.broadcasted_iota(jnp.int32, q.shape, 1) < DA_HEAD_DIM
    zero = jnp.zeros_like(q)
    q2_scr[0:bq, :] = jnp.where(first, q, zero)
    q2_scr[bq:2 * bq, :] = jnp.where(first, zero, q)
    for cc in range(n_chunk):
        m_scr[cc][...] = jnp.full(m_scr[cc].shape, NEG, F32)
        mlast_scr[cc][...] = jnp.full(mlast_scr[cc].shape, NEG, F32)
        acc_scr[cc][...] = jnp.zeros(acc_scr[cc].shape, F32)
    c_neg = cst_ref[hd, 0]
    c_pos = cst_ref[hd, 1]

    def logits(k_blk, buf, c, bias=None, first=False):
        nk = k_blk.shape[0]
        for cc in range(n_chunk):
            s = lax.dot_general(k_blk, q2_scr[cc * COL_CHUNK:(cc + 1) * COL_CHUNK, :], nt,
                                preferred_element_type=F32)
            if bias is not None:
                if bq >= COL_CHUNK:
                    b0 = (cc * COL_CHUNK) % bq
                    s = s + bias[:, b0:b0 + COL_CHUNK]
                else:
                    s = s + jnp.concatenate([bias[...]] * (COL_CHUNK // bq), axis=1)
            s_max = jnp.max(s, axis=0, keepdims=True) + c
            if not eager_probs:
                blk_scr[buf][cc][0:nk, :] = s
                row_scr[buf][cc][...] = s_max
                continue
            m_run = m_scr[cc][...]
            scale = s_max if first else m_run
            blk_scr[buf][cc][0:nk, :] = jnp.exp2(s - (scale - c)).astype(BF16)
            row_scr[buf][cc][...] = jnp.exp2(mlast_scr[cc][...] - scale)
            mlast_scr[cc][...] = scale
            m_scr[cc][...] = s_max if first else jnp.maximum(m_run, s_max)

    def softmax_pv(vt, c, buf):
        nk = vt.shape[1]
        kd = min(nk, MXU_DEPTH)
        for cc in range(n_chunk):
            if eager_probs:
                acc = acc_scr[cc][...] * row_scr[buf][cc][...]
                for k0 in range(0, nk, kd):
                    acc = acc + jnp.dot(vt[:, k0:k0 + kd], blk_scr[buf][cc][k0:k0 + kd, :],
                                        preferred_element_type=F32)
            else:
                m_old = m_scr[cc][...]
                m_new = jnp.maximum(m_old, row_scr[buf][cc][...])
                shift = m_new - c
                acc = acc_scr[cc][...] * jnp.exp2(m_old - m_new)
                for k0 in range(0, nk, kd):
                    p = jnp.concatenate(
                        [jnp.exp2(blk_scr[buf][cc][r0:r0 + ROW_PIECE, :] - shift).astype(BF16)
                         for r0 in range(k0, k0 + kd, ROW_PIECE)], axis=0)
                    acc = acc + jnp.dot(vt[:, k0:k0 + kd], p, preferred_element_type=F32)
                m_scr[cc][...] = m_new
            acc_scr[cc][...] = acc

    n_far = nkb - n_near
    lo = 0 if meta_queries else jnp.clip(qi - n_near // 2, 0, nkb - n_near)
    tm = 0 if meta_queries else jnp.minimum(qi, 1)
    key_blk = lambda kb: k_ref[0, pl.ds(pl.multiple_of(kb * bk, bk), bk), :]
    far_kb = lambda j: jnp.where(j < lo, j, j + n_near)

    def issue_logits(t, buf, first=False):
        if isinstance(t, int) and t == nkb:
            logits(k_ref[0, seq:seq + META_BLK, :], buf, 0.0, btm_ref.at[0, tm])
        elif isinstance(t, int) and t >= n_far:
            kb = lo + (t - n_far)
            logits(key_blk(kb), buf, 0.0, bt_ref.at[0, 0 if meta_queries else kb - qi + n_near - 1])
        else:
            logits(key_blk(far_kb(t)), buf, jnp.where(t < lo, c_neg, c_pos), first=first)

    def finish(t, buf):
        if isinstance(t, int) and t == nkb:
            softmax_pv(vtm_scr[...], 0.0, buf)
        elif isinstance(t, int) and t >= n_far:
            softmax_pv(vt_scr[lo + (t - n_far)], 0.0, buf)
        else:
            softmax_pv(vt_scr[far_kb(t)], jnp.where(t < lo, c_neg, c_pos), buf)

    issue_logits(0, 0, first=True)
    n_group = (n_far - 1) // LOGIT_BUFS

    def group_body(i, carry):
        for u in range(LOGIT_BUFS):
            issue_logits(LOGIT_BUFS * i + u + 1, (u + 1) % LOGIT_BUFS)
            finish(LOGIT_BUFS * i + u, u)
        return carry

    lax.fori_loop(0, n_group, group_body, 0)
    for t in range(LOGIT_BUFS * n_group, nkb + 1):
        if t < nkb:
            issue_logits(t + 1, (t + 1) % LOGIT_BUFS)
        finish(t, t % LOGIT_BUFS)

    lp = lamp_ref[...]
    lam = (jnp.exp(jnp.sum(lp[0:1] * lp[1:2], axis=1, keepdims=True))
           - jnp.exp(jnp.sum(lp[2:3] * lp[3:4], axis=1, keepdims=True)) + lambda_init)
    on = jnp.concatenate([acc_scr[cc][0:LANES, :] / acc_scr[cc][LANES:LANES + 1, :] for cc in range(n_chunk)],
                         axis=1)
    ot = on[:, :bq] - lam * on[:, bq:]
    y = ot * lax.rsqrt(jnp.mean(ot * ot, axis=0, keepdims=True) + RMS_EPS) * g_ref[...]
    o_ref[0] = (y * (1.0 - lambda_init)).T.astype(BF16)


def _t5_bucket(rel):
    nb = T5_BUCKETS // 2
    max_exact = nb // 2
    ret = jnp.where(rel > 0, nb, 0)
    n = jnp.abs(rel)
    nf = jnp.maximum(n, 1).astype(jnp.float32)
    large = max_exact + (jnp.log(nf / max_exact) / math.log(T5_MAX_DIST / max_exact)
                         * (nb - max_exact)).astype(jnp.int32)
    large = jnp.minimum(large, nb - 1)
    return ret + jnp.where(n < max_exact, n, large)


def _da_tables(rel_table, seq):
    blk = ATT_BLK
    n_head = rel_table.shape[1]
    reach = 2 * blk - 1
    rel_table = rel_table.astype(F32) * LOG2E
    line = rel_table[_t5_bucket(jnp.arange(-reach, reach + 1, dtype=jnp.int32))].T
    val = lambda rel: line[:, np.clip(np.asarray(rel) + reach, 0, 2 * reach)]
    cst = jnp.stack([rel_table[T5_BUCKETS // 2 - 1], rel_table[T5_BUCKETS - 1]], axis=1)
    qq = np.arange(blk)[None, :]
    period = 2 * blk
    off = np.arange(period)
    off = np.where(off < blk, off, off - period)
    flat = lambda col: jnp.broadcast_to(cst[:, col, None, None], (n_head, blk, blk))
    bt = jnp.stack([flat(0)] + [_toeplitz(val(d * blk - off), blk, blk) for d in (-1, 0, 1)] + [flat(1)],
                   axis=1)
    jm = np.arange(N_META)[:, None]
    first = val(jm - (qq + N_META))
    later = jnp.broadcast_to(cst[:, 0, None, None], first.shape)
    pad_keys = lambda cols, n: jnp.full((n_head, n, META_BLK - N_META, cols), NEG, F32)
    btm = jnp.concatenate([jnp.stack([first, later], axis=1), pad_keys(blk, 2)], axis=2)
    pad_cols = lambda t: jnp.concatenate(
        [t, jnp.broadcast_to(t[..., -1:], t.shape[:-1] + (META_BLK - N_META,))], axis=-1)
    im = np.arange(N_META)[None, :]
    bt_q = pad_cols(val(np.arange(blk)[:, None] + N_META - im))[:, None]
    btm_q = jnp.concatenate([pad_cols(val(jm - im))[:, None], pad_keys(META_BLK, 1)], axis=2)
    return cst, bt, btm, bt_q, btm_q


def _da_logits_bounded(stats, rel_table):
    st = jnp.max(stats, axis=0)
    n_half = 2 * DA_HEADS
    dot_bound = jnp.sqrt(st[0, :n_half] * st[1, :n_half]).reshape(DA_HEADS, 2) * NORM_SLACK
    bias_bound = jnp.max(jnp.abs(rel_table.astype(F32)), axis=0) * LOG2E
    logit_bound = jnp.max(jnp.max(dot_bound, axis=1) + bias_bound)
    return (logit_bound <= EAGER_HALF_RANGE) & (jnp.max(st[2]) <= EAGER_V_MAX)


def _da_call(qkv, prev_out, cst, bt, btm, lam_p, g_col, *, seq, meta_queries, lambda_init, eager_probs):
    b, tp, _ = qkv.shape
    nkb = seq // ATT_BLK
    bq = META_BLK if meta_queries else ATT_BLK
    nq = 1 if meta_queries else nkb
    q_row0 = seq // bq if meta_queries else 0
    n_t, n_tm = bt.shape[1], btm.shape[1]
    in_specs = [
        pl.BlockSpec(memory_space=pltpu.SMEM),
        pl.BlockSpec((1, bq, LANES), lambda i, h, j: (i, q_row0 + j, h)),
        pl.BlockSpec((1, tp, LANES), lambda i, h, j: (i, 0, DA_HEADS + h)),
        pl.BlockSpec((1, tp, LANES), lambda i, h, j: (i, 0, 2 * DA_HEADS + h)),
        pl.BlockSpec((1, n_t, ATT_BLK, bq), lambda i, h, j: (h, 0, 0, 0)),
        pl.BlockSpec((1, n_tm, META_BLK, bq), lambda i, h, j: (h, 0, 0, 0)),
        pl.BlockSpec((4, DA_HEAD_DIM), lambda i, h, j: (0, 0)),
        pl.BlockSpec((LANES, 1), lambda i, h, j: (0, 0)),
    ]
    args = [cst, qkv, qkv, qkv, bt, btm, lam_p, g_col]
    aliases = {}
    if prev_out is not None:
        in_specs.append(pl.BlockSpec(memory_space=pl.ANY))
        args.append(prev_out)
        aliases = {len(args) - 1: 0}
    kern = functools.partial(_da_kernel, bq=bq, nkb=nkb, n_near=(n_t + 1) // 2, seq=seq,
                             meta_queries=meta_queries, lambda_init=lambda_init, eager_probs=eager_probs)
    if prev_out is not None:
        body = kern
        kern = lambda *refs: body(*refs[:8], *refs[9:])
    return pl.pallas_call(
        kern,
        out_shape=jax.ShapeDtypeStruct((b, tp, D_MODEL), BF16),
        grid=(b, DA_HEADS, nq),
        in_specs=in_specs,
        out_specs=pl.BlockSpec((1, bq, LANES), lambda i, h, j: (i, q_row0 + j, h)),
        scratch_shapes=[pltpu.VMEM((nkb, LANES + ONES_ROWS, ATT_BLK), BF16),
                        pltpu.VMEM((LANES + ONES_ROWS, META_BLK), BF16), pltpu.VMEM((2 * bq, LANES), BF16)]
        + [pltpu.VMEM(shape, dt)
           for shape, dt in ([((1, COL_CHUNK), F32), ((1, COL_CHUNK), F32), ((LANES + ONES_ROWS, COL_CHUNK), F32)]
                             + [((ATT_BLK, COL_CHUNK), BF16 if eager_probs else F32)] * LOGIT_BUFS
                             + [((1, COL_CHUNK), F32)] * LOGIT_BUFS)
           for _ in range(2 * bq // COL_CHUNK)],
        input_output_aliases=aliases,
        compiler_params=_params(("parallel", "parallel", "arbitrary")),
        name="da_meta_queries" if meta_queries else "da_attention",
    )(*args)


def kernel(x, meta_tokens, norm_g, ffn_w_gate, ffn_w_up, ffn_w_down, na_w_qkv, na_b_qkv, na_w_o, na_b_o, na_rpb, na_meta_bias, da_w_qkv, da_w_o, da_lambda, da_subln_g, t5_rel_bias):
    b, seq, d = x.shape
    depth = norm_g.shape[0]
    assert d == D_MODEL and seq % ATT_BLK == 0 and seq % GRID_W == 0
    tp = seq + META_BLK
    assert (b * tp) % ROW_TILE == 0
    meta = jnp.broadcast_to(meta_tokens[None].astype(x.dtype), (b, N_META, d))
    pad = jnp.zeros((b, META_BLK - N_META, d), x.dtype)
    h = jnp.concatenate([x, meta, pad], axis=1).reshape(b * tp, d)

    vec = lambda v: v.reshape(1, -1).astype(F32)
    zero_qkv_bias = jnp.zeros((1, 3 * d), F32)
    zero_o_bias = jnp.zeros((1, d), F32)
    da_cst = _da_tables(t5_rel_bias, seq)

    for i in range(depth):
        g = norm_g[i]
        j = i // N_MIXERS
        h = _ffn(h, vec(g[0]), vec(g[1]), ffn_w_gate[i, 0].astype(BF16), ffn_w_up[i, 0].astype(BF16),
                 ffn_w_down[i, 0].astype(BF16))
        if i % N_MIXERS == 0:
            qkv = _qkv(h, vec(g[2]), na_w_qkv[j].astype(BF16), vec(na_b_qkv[j]), (D_MODEL // NA_HEADS) ** -0.5)
            nab, mb = _na_tables(na_rpb[j], na_meta_bias[j], seq // GRID_W)
            a = _na_attention(qkv.reshape(b, tp, 3 * d), nab, mb, seq)
            w_o, b_o = na_w_o[j], vec(na_b_o[j])
        else:
            lambda_init = 0.8 - 0.6 * math.exp(-0.3 * i)
            qkv, qkv_stats = _qkv(h, vec(g[2]), da_w_qkv[j].astype(BF16), zero_qkv_bias,
                                  DA_HEAD_DIM ** -0.5 * LOG2E, with_stats=True)
            qkv = qkv.reshape(b, tp, 3 * d)
            cst, bt, btm, bt_q, btm_q = da_cst
            g_col = da_subln_g[j].reshape(LANES, 1).astype(F32)
            lam_p = da_lambda[j].astype(F32)

            def attend(eager_probs):
                a = _da_call(qkv, None, cst, bt, btm, lam_p, g_col, seq=seq, meta_queries=False,
                             lambda_init=lambda_init, eager_probs=eager_probs)
                return _da_call(qkv, a, cst, bt_q, btm_q, lam_p, g_col, seq=seq, meta_queries=True,
                                lambda_init=lambda_init, eager_probs=eager_probs)

            a = lax.cond(_da_logits_bounded(qkv_stats, t5_rel_bias), functools.partial(attend, True),
                         functools.partial(attend, False))
            w_o, b_o = da_w_o[j], zero_o_bias
        h = _mix_ffn(a, h.reshape(b, tp, d), w_o.astype(BF16), b_o, vec(g[3]), vec(g[4]), vec(g[5]),
                     ffn_w_gate[i, 1].astype(BF16), ffn_w_up[i, 1].astype(BF16), ffn_w_down[i, 1].astype(BF16),
                     out_rows=seq if i == depth - 1 else None)
        if i < depth - 1:
            h = h.reshape(b * tp, d)
    return h
```

```python
import functools
import math

import numpy as np
import jax
import jax.numpy as jnp
from jax import lax
from jax.experimental import pallas as pl
from jax.experimental.pallas import tpu as pltpu

F32 = jnp.float32
BF16 = jnp.bfloat16

D_MODEL = 1024
N_META = 16
GRID_W = 64
NA_KH = 8
NA_KW = 16
NA_HEADS = 16
DA_HEADS = 8
DA_HEAD_DIM = 64
T5_BUCKETS = 32
T5_MAX_DIST = 128
D_FF = 2816
FFN_RES = 0.5
RMS_EPS = 1e-6
N_MIXERS = 2

LANES = 128
META_BLK = 128
ROW_TILES = (1280, 640, 512, 128)
MIX_ROW_TILES = (1024, 640, 512, 128)
FFN_HIDDEN_CHUNK = 256
ATT_BLK = 512
COL_CHUNK = 256
ROW_PIECE = 64
NA_ROWS_PER_STEP = 8
MXU_DEPTH = 256
LOGIT_BUFS = 3
ONES_ROWS = 16
EAGER_HALF_RANGE = 45.0
NORM_SLACK = 1.02
EAGER_V_MAX = 2.0 ** 20
LOG2E = math.log2(math.e)
NEG = -1e30
VMEM_LIMIT = 56 * 1024 * 1024


def _rms(x, g):
    return x * lax.rsqrt(jnp.mean(x * x, axis=-1, keepdims=True) + RMS_EPS) * g


def _params(sem):
    return pltpu.CompilerParams(dimension_semantics=sem, vmem_limit_bytes=VMEM_LIMIT)


def _row_tile(rows, candidates):
    return next(t for t in candidates if rows % t == 0)


def _resident(shape):
    return pl.BlockSpec(shape, lambda *_: (0,) * len(shape), pipeline_mode=pl.Buffered(1))


def _swiglu_step(x, gin_ref, gout_ref, wg_ref, wu_ref, wd_ref):
    xn = _rms(x, gin_ref[...]).astype(BF16)
    acc = None
    for c in range(D_FF // FFN_HIDDEN_CHUNK):
        sl = slice(c * FFN_HIDDEN_CHUNK, (c + 1) * FFN_HIDDEN_CHUNK)
        g = jnp.dot(xn, wg_ref[:, sl], preferred_element_type=F32)
        u = jnp.dot(xn, wu_ref[:, sl], preferred_element_type=F32)
        hid = (g * jax.nn.sigmoid(g) * u).astype(BF16)
        d = jnp.dot(hid, wd_ref[sl, :], preferred_element_type=F32)
        acc = d if acc is None else acc + d
    return x + FFN_RES * _rms(acc, gout_ref[...])


def _ffn_kernel(h_ref, gin_ref, gout_ref, wg_ref, wu_ref, wd_ref, o_ref):
    o_ref[...] = _swiglu_step(h_ref[...], gin_ref, gout_ref, wg_ref, wu_ref, wd_ref)


def _ffn_weight_specs():
    return [_resident((1, D_MODEL)), _resident((1, D_MODEL)),
            _resident((D_MODEL, D_FF)), _resident((D_MODEL, D_FF)), _resident((D_FF, D_MODEL))]


def _ffn(h, g_in, g_out, wg, wu, wd):
    mt = h.shape[0]
    tile = _row_tile(mt, ROW_TILES)
    row = pl.BlockSpec((tile, D_MODEL), lambda i: (i, 0))
    return pl.pallas_call(
        _ffn_kernel,
        out_shape=jax.ShapeDtypeStruct((mt, D_MODEL), F32),
        grid=(mt // tile,),
        in_specs=[row] + _ffn_weight_specs(),
        out_specs=row,
        compiler_params=_params(("parallel",)),
        name="ffn",
    )(h, g_in, g_out, wg, wu, wd)


def _mix_ffn_kernel(a_ref, h_ref, wo_ref, bo_ref, gm_ref, gin_ref, gout_ref, wg_ref, wu_ref, wd_ref, o_ref):
    m = jnp.dot(a_ref[...], wo_ref[...], preferred_element_type=F32) + bo_ref[...]
    h1 = h_ref[...] + _rms(m, gm_ref[...])
    o_ref[...] = _swiglu_step(h1, gin_ref, gout_ref, wg_ref, wu_ref, wd_ref)


def _mix_ffn(a, h, wo, bo, g_mix, g_in, g_out, wg, wu, wd, out_rows=None):
    b, tp, _ = h.shape
    rows = tp if out_rows is None else out_rows
    tile = _row_tile(rows, MIX_ROW_TILES)
    blk = lambda width: pl.BlockSpec((None, tile, width), lambda i, j: (i, j, 0))
    return pl.pallas_call(
        _mix_ffn_kernel,
        out_shape=jax.ShapeDtypeStruct((b, rows, D_MODEL), F32),
        grid=(b, rows // tile),
        in_specs=[blk(D_MODEL), blk(D_MODEL), _resident((D_MODEL, D_MODEL)), _resident((1, D_MODEL)),
                  _resident((1, D_MODEL))] + _ffn_weight_specs(),
        out_specs=blk(D_MODEL),
        compiler_params=_params(("parallel", "parallel")),
        name="mix_ffn",
    )(a, h, wo, bo, g_mix, g_in, g_out, wg, wu, wd)


def _qkv_kernel(h_ref, g_ref, w_ref, b_ref, *rest, q_scale, with_stats):
    if with_stats:
        ind_ref, o_ref, st_ref = rest
    else:
        (o_ref,) = rest
    xn = _rms(h_ref[...], g_ref[...]).astype(BF16)
    stats = []
    for j in range(3):
        sl = slice(j * D_MODEL, (j + 1) * D_MODEL)
        y = jnp.dot(xn, w_ref[:, sl], preferred_element_type=F32) + b_ref[:, sl]
        if j == 0:
            y = y * q_scale
        o_ref[:, sl] = y.astype(BF16)
        if with_stats and j < 2:
            norm2 = jnp.dot((y * y).astype(BF16), ind_ref[...], preferred_element_type=F32)
            stats.append(jnp.max(norm2, axis=0, keepdims=True))
        elif with_stats:
            a = jnp.abs(y)
            fold = a[:, 0:LANES]
            for t in range(1, D_MODEL // LANES):
                fold = jnp.maximum(fold, a[:, t * LANES:(t + 1) * LANES])
            stats.append(jnp.max(fold, axis=0, keepdims=True))
    if with_stats:
        st_ref[0] = jnp.concatenate(stats + [jnp.zeros((8 - len(stats), LANES), F32)], axis=0)


def _qkv(h, g, w, b, q_scale, with_stats=False):
    mt = h.shape[0]
    tile = _row_tile(mt, ROW_TILES)
    n_tile = mt // tile
    in_specs = [pl.BlockSpec((tile, D_MODEL), lambda i: (i, 0)), _resident((1, D_MODEL)),
                _resident((D_MODEL, 3 * D_MODEL)), _resident((1, 3 * D_MODEL))]
    out_shape = [jax.ShapeDtypeStruct((mt, 3 * D_MODEL), BF16)]
    out_specs = [pl.BlockSpec((tile, 3 * D_MODEL), lambda i: (i, 0))]
    args = [h, g, w, b]
    if with_stats:
        ind = (np.arange(D_MODEL)[:, None] // DA_HEAD_DIM == np.arange(LANES)[None, :])
        args.append(jnp.asarray(ind, BF16))
        in_specs.append(_resident((D_MODEL, LANES)))
        out_shape.append(jax.ShapeDtypeStruct((n_tile, 8, LANES), F32))
        out_specs.append(pl.BlockSpec((1, 8, LANES), lambda i: (i, 0, 0)))
    out = pl.pallas_call(
        functools.partial(_qkv_kernel, q_scale=q_scale, with_stats=with_stats),
        out_shape=out_shape,
        grid=(n_tile,),
        in_specs=in_specs,
        out_specs=out_specs,
        compiler_params=_params(("parallel",)),
        name="qkv_proj",
    )(*args)
    return out if with_stats else out[0]


def _na_kernel(q_ref, k_ref, v_ref, nab_ref, mb_ref, o_ref, *, rows, seq):
    kh = NA_KH
    lane = lax.broadcasted_iota(jnp.int32, (GRID_W, LANES), 1)
    head0 = lane < (LANES // 2)
    k_meta = k_ref[0, seq:seq + META_BLK, :]
    v_meta = v_ref[0, seq:seq + META_BLK, :]
    mb = jnp.concatenate([jnp.broadcast_to(mb_ref[0, 0], (GRID_W, LANES)),
                          jnp.broadcast_to(mb_ref[0, 1], (GRID_W, LANES))], axis=0)
    nt = (((1,), (1,)), ((), ()))

    def split_heads(q):
        zero = jnp.zeros_like(q)
        return jnp.concatenate([jnp.where(head0[:q.shape[0]], q, zero),
                                jnp.where(head0[:q.shape[0]], zero, q)], axis=0)

    def row_fn(r):
        rs = jnp.clip(r - kh // 2, 0, rows - kh)
        dy0 = rs - r + (NA_KH - 1)
        q2 = split_heads(q_ref[0, pl.ds(pl.multiple_of(r * GRID_W, GRID_W), GRID_W), :])
        kbase = pl.multiple_of(rs * GRID_W, GRID_W)
        k_win = k_ref[0, pl.ds(kbase, kh * GRID_W), :]
        v_win = v_ref[0, pl.ds(kbase, kh * GRID_W), :]
        bias = jnp.concatenate([nab_ref[0, 0, dy0], nab_ref[0, 1, dy0]], axis=0)
        s = lax.dot_general(q2, k_win, nt, preferred_element_type=F32) + bias
        sm = lax.dot_general(q2, k_meta, nt, preferred_element_type=F32) + mb
        m = jnp.maximum(jnp.max(s, axis=-1, keepdims=True), jnp.max(sm, axis=-1, keepdims=True))
        p = jnp.exp(s - m)
        pm = jnp.exp(sm - m)
        l = jnp.sum(p, axis=-1, keepdims=True) + jnp.sum(pm, axis=-1, keepdims=True)
        o2 = (jnp.dot(p.astype(BF16), v_win, preferred_element_type=F32)
              + jnp.dot(pm.astype(BF16), v_meta, preferred_element_type=F32)) / l
        o = jnp.where(head0, o2[:GRID_W], o2[GRID_W:])
        o_ref[0, pl.ds(pl.multiple_of(r * GRID_W, GRID_W), GRID_W), :] = o.astype(BF16)

    def row_group(i, carry):
        for u in range(NA_ROWS_PER_STEP):
            row_fn(i * NA_ROWS_PER_STEP + u)
        return carry

    lax.fori_loop(0, rows // NA_ROWS_PER_STEP, row_group, 0)

    qm = q_ref[0, seq:seq + META_BLK, :]
    lane_m = lax.broadcasted_iota(jnp.int32, (META_BLK, LANES), 1) < (LANES // 2)
    zero = jnp.zeros_like(qm)
    q2 = jnp.concatenate([jnp.where(lane_m, qm, zero), jnp.where(lane_m, zero, qm)], axis=0)
    mbm = jnp.concatenate([jnp.broadcast_to(mb_ref[0, 0], (META_BLK, LANES)),
                           jnp.broadcast_to(mb_ref[0, 1], (META_BLK, LANES))], axis=0)
    sm = lax.dot_general(q2, k_meta, nt, preferred_element_type=F32) + mbm
    pm = jnp.exp(sm - jnp.max(sm, axis=-1, keepdims=True))
    o2 = jnp.dot(pm.astype(BF16), v_meta, preferred_element_type=F32) / jnp.sum(pm, axis=-1, keepdims=True)
    o_ref[0, seq:seq + META_BLK, :] = jnp.where(lane_m, o2[:META_BLK], o2[META_BLK:]).astype(BF16)


def _toeplitz(w, n, m):
    period = w.shape[-1]
    width = period - 1
    assert m <= width
    reps = -(-(n * width) // period)
    flat = jnp.tile(w, (1,) * (w.ndim - 1) + (reps,))[..., :n * width]
    return flat.reshape(w.shape[:-1] + (n, width))[..., :m]


def _na_tables(rpb, meta_bias, rows):
    kh = NA_KH
    qc = np.arange(GRID_W)[:, None]
    kc = np.arange(GRID_W)[None, :]
    cs = np.clip(qc - NA_KW // 2, 0, GRID_W - NA_KW)
    valid = (kc >= cs) & (kc < cs + NA_KW)
    period = 2 * GRID_W
    off = np.arange(period)
    off = np.where(off < GRID_W, off, off - period)
    w = rpb[:, :, np.clip(off + NA_KW - 1, 0, 2 * NA_KW - 2)]
    t = jnp.where(valid, _toeplitz(w, GRID_W, GRID_W), NEG)
    t = jnp.stack([t[:, dy0:dy0 + kh].transpose(0, 2, 1, 3) for dy0 in range(kh)], axis=1)
    t = t.reshape(NA_HEADS // 2, 2, kh, GRID_W, kh * GRID_W)
    mb = jnp.full((NA_HEADS, LANES), NEG, F32).at[:, :N_META].set(meta_bias)
    return t.astype(F32), mb.reshape(NA_HEADS // 2, 2, 1, LANES)


def _na_attention(qkv, nab, mb, seq):
    b, tp, _ = qkv.shape
    rows = seq // GRID_W
    n_slab = D_MODEL // LANES
    slab = lambda off: pl.BlockSpec((1, tp, LANES), lambda i, s: (i, 0, off + s))
    return pl.pallas_call(
        functools.partial(_na_kernel, rows=rows, seq=seq),
        out_shape=jax.ShapeDtypeStruct((b, tp, D_MODEL), BF16),
        grid=(b, n_slab),
        in_specs=[slab(0), slab(n_slab), slab(2 * n_slab),
                  pl.BlockSpec((1, 2, NA_KH, GRID_W, NA_KH * GRID_W), lambda i, s: (s, 0, 0, 0, 0)),
                  pl.BlockSpec((1, 2, 1, LANES), lambda i, s: (s, 0, 0, 0))],
        out_specs=slab(0),
        compiler_params=_params(("parallel", "parallel")),
        name="na_attention",
    )(qkv, qkv, qkv, nab, mb)


def _da_kernel(cst_ref, q_ref, k_ref, v_ref, bt_ref, btm_ref, lamp_ref, g_ref, o_ref,
               vt_scr, vtm_scr, q2_scr, *chunk_scr, bq, nkb, n_near, seq, meta_queries, lambda_init,
               eager_probs):
    bk = ATT_BLK
    n_chunk = 2 * bq // COL_CHUNK
    groups = [chunk_scr[i * n_chunk:(i + 1) * n_chunk] for i in range(3 + 2 * LOGIT_BUFS)]
    m_scr, mlast_scr, acc_scr = groups[:3]
    blk_scr, row_scr = groups[3:3 + LOGIT_BUFS], groups[3 + LOGIT_BUFS:]
    hd = pl.program_id(1)
    qi = pl.program_id(2)
    nt = (((1,), (1,)), ((), ()))

    @pl.when(qi == 0)
    def _():
        for kb in range(nkb):
            vt_scr[kb, 0:LANES, :] = v_ref[0, kb * bk:(kb + 1) * bk, :].astype(F32).T.astype(BF16)
            vt_scr[kb, LANES:, :] = jnp.ones((ONES_ROWS, bk), BF16)
        vtm_scr[0:LANES, :] = v_ref[0, seq:seq + META_BLK, :].astype(F32).T.astype(BF16)
        vtm_scr[LANES:, :] = jnp.ones((ONES_ROWS, META_BLK), BF16)

    q = q_ref[0]
    first = lax.broadcasted_iota(jnp.int32, q.shape, 1) < DA_HEAD_DIM
    zero = jnp.zeros_like(q)
    q2_scr[0:bq, :] = jnp.where(first, q, zero)
    q2_scr[bq:2 * bq, :] = jnp.where(first, zero, q)
    for cc in range(n_chunk):
        m_scr[cc][...] = jnp.full(m_scr[cc].shape, NEG, F32)
        mlast_scr[cc][...] = jnp.full(mlast_scr[cc].shape, NEG, F32)
        acc_scr[cc][...] = jnp.zeros(acc_scr[cc].shape, F32)
    c_neg = cst_ref[hd, 0]
    c_pos = cst_ref[hd, 1]

    def logits(k_blk, buf, c, bias=None, first=False):
        nk = k_blk.shape[0]
        for cc in range(n_chunk):
            s = lax.dot_general(k_blk, q2_scr[cc * COL_CHUNK:(cc + 1) * COL_CHUNK, :], nt,
                                preferred_element_type=F32)
            if bias is not None:
                if bq >= COL_CHUNK:
                    b0 = (cc * COL_CHUNK) % bq
                    s = s + bias[:, b0:b0 + COL_CHUNK]
                else:
                    s = s + jnp.concatenate([bias[...]] * (COL_CHUNK // bq), axis=1)
            s_max = jnp.max(s, axis=0, keepdims=True) + c
            if not eager_probs:
                blk_scr[buf][cc][0:nk, :] = s
                row_scr[buf][cc][...] = s_max
                continue
            m_run = m_scr[cc][...]
            scale = s_max if first else m_run
            blk_scr[buf][cc][0:nk, :] = jnp.exp2(s - (scale - c)).astype(BF16)
            row_scr[buf][cc][...] = jnp.exp2(mlast_scr[cc][...] - scale)
            mlast_scr[cc][...] = scale
            m_scr[cc][...] = s_max if first else jnp.maximum(m_run, s_max)

    def softmax_pv(vt, c, buf):
        nk = vt.shape[1]
        kd = min(nk, MXU_DEPTH)
        for cc in range(n_chunk):
            if eager_probs:
                acc = acc_scr[cc][...] * row_scr[buf][cc][...]
                for k0 in range(0, nk, kd):
                    acc = acc + jnp.dot(vt[:, k0:k0 + kd], blk_scr[buf][cc][k0:k0 + kd, :],
                                        preferred_element_type=F32)
            else:
                m_old = m_scr[cc][...]
                m_new = jnp.maximum(m_old, row_scr[buf][cc][...])
                shift = m_new - c
                acc = acc_scr[cc][...] * jnp.exp2(m_old - m_new)
                for k0 in range(0, nk, kd):
                    p = jnp.concatenate(
                        [jnp.exp2(blk_scr[buf][cc][r0:r0 + ROW_PIECE, :] - shift).astype(BF16)
                         for r0 in range(k0, k0 + kd, ROW_PIECE)], axis=0)
                    acc = acc + jnp.dot(vt[:, k0:k0 + kd], p, preferred_element_type=F32)
                m_scr[cc][...] = m_new
            acc_scr[cc][...] = acc

    n_far = nkb - n_near
    lo = 0 if meta_queries else jnp.clip(qi - n_near // 2, 0, nkb - n_near)
    tm = 0 if meta_queries else jnp.minimum(qi, 1)
    key_blk = lambda kb: k_ref[0, pl.ds(pl.multiple_of(kb * bk, bk), bk), :]
    far_kb = lambda j: jnp.where(j < lo, j, j + n_near)

    def issue_logits(t, buf, first=False):
        if isinstance(t, int) and t == nkb:
            logits(k_ref[0, seq:seq + META_BLK, :], buf, 0.0, btm_ref.at[0, tm])
        elif isinstance(t, int) and t >= n_far:
            kb = lo + (t - n_far)
            logits(key_blk(kb), buf, 0.0, bt_ref.at[0, 0 if meta_queries else kb - qi + n_near - 1])
        else:
            logits(key_blk(far_kb(t)), buf, jnp.where(t < lo, c_neg, c_pos), first=first)

    def finish(t, buf):
        if isinstance(t, int) and t == nkb:
            softmax_pv(vtm_scr[...], 0.0, buf)
        elif isinstance(t, int) and t >= n_far:
            softmax_pv(vt_scr[lo + (t - n_far)], 0.0, buf)
        else:
            softmax_pv(vt_scr[far_kb(t)], jnp.where(t < lo, c_neg, c_pos), buf)

    issue_logits(0, 0, first=True)
    n_group = (n_far - 1) // LOGIT_BUFS

    def group_body(i, carry):
        for u in range(LOGIT_BUFS):
            issue_logits(LOGIT_BUFS * i + u + 1, (u + 1) % LOGIT_BUFS)
            finish(LOGIT_BUFS * i + u, u)
        return carry

    lax.fori_loop(0, n_group, group_body, 0)
    for t in range(LOGIT_BUFS * n_group, nkb + 1):
        if t < nkb:
            issue_logits(t + 1, (t + 1) % LOGIT_BUFS)
        finish(t, t % LOGIT_BUFS)

    lp = lamp_ref[...]
    lam = (jnp.exp(jnp.sum(lp[0:1] * lp[1:2], axis=1, keepdims=True))
           - jnp.exp(jnp.sum(lp[2:3] * lp[3:4], axis=1, keepdims=True)) + lambda_init)
    on = jnp.concatenate([acc_scr[cc][0:LANES, :] / acc_scr[cc][LANES:LANES + 1, :] for cc in range(n_chunk)],
                         axis=1)
    ot = on[:, :bq] - lam * on[:, bq:]
    y = ot * lax.rsqrt(jnp.mean(ot * ot, axis=0, keepdims=True) + RMS_EPS) * g_ref[...]
    o_ref[0] = (y * (1.0 - lambda_init)).T.astype(BF16)


def _t5_bucket(rel):
    nb = T5_BUCKETS // 2
    max_exact = nb // 2
    ret = jnp.where(rel > 0, nb, 0)
    n = jnp.abs(rel)
    nf = jnp.maximum(n, 1).astype(jnp.float32)
    large = max_exact + (jnp.log(nf / max_exact) / math.log(T5_MAX_DIST / max_exact)
                         * (nb - max_exact)).astype(jnp.int32)
    large = jnp.minimum(large, nb - 1)
    return ret + jnp.where(n < max_exact, n, large)


def _da_tables(rel_table, seq):
    blk = ATT_BLK
    n_head = rel_table.shape[1]
    reach = 2 * blk - 1
    rel_table = rel_table.astype(F32) * LOG2E
    line = rel_table[_t5_bucket(jnp.arange(-reach, reach + 1, dtype=jnp.int32))].T
    val = lambda rel: line[:, np.clip(np.asarray(rel) + reach, 0, 2 * reach)]
    cst = jnp.stack([rel_table[T5_BUCKETS // 2 - 1], rel_table[T5_BUCKETS - 1]], axis=1)
    qq = np.arange(blk)[None, :]
    period = 2 * blk
    off = np.arange(period)
    off = np.where(off < blk, off, off - period)
    flat = lambda col: jnp.broadcast_to(cst[:, col, None, None], (n_head, blk, blk))
    bt = jnp.stack([flat(0)] + [_toeplitz(val(d * blk - off), blk, blk) for d in (-1, 0, 1)] + [flat(1)],
                   axis=1)
    jm = np.arange(N_META)[:, None]
    first = val(jm - (qq + N_META))
    later = jnp.broadcast_to(cst[:, 0, None, None], first.shape)
    pad_keys = lambda cols, n: jnp.full((n_head, n, META_BLK - N_META, cols), NEG, F32)
    btm = jnp.concatenate([jnp.stack([first, later], axis=1), pad_keys(blk, 2)], axis=2)
    pad_cols = lambda t: jnp.concatenate(
        [t, jnp.broadcast_to(t[..., -1:], t.shape[:-1] + (META_BLK - N_META,))], axis=-1)
    im = np.arange(N_META)[None, :]
    bt_q = pad_cols(val(np.arange(blk)[:, None] + N_META - im))[:, None]
    btm_q = jnp.concatenate([pad_cols(val(jm - im))[:, None], pad_keys(META_BLK, 1)], axis=2)
    return cst, bt, btm, bt_q, btm_q


def _da_logits_bounded(stats, rel_table):
    st = jnp.max(stats, axis=0)
    n_half = 2 * DA_HEADS
    dot_bound = jnp.sqrt(st[0, :n_half] * st[1, :n_half]).reshape(DA_HEADS, 2) * NORM_SLACK
    bias_bound = jnp.max(jnp.abs(rel_table.astype(F32)), axis=0) * LOG2E
    logit_bound = jnp.max(jnp.max(dot_bound, axis=1) + bias_bound)
    return (logit_bound <= EAGER_HALF_RANGE) & (jnp.max(st[2]) <= EAGER_V_MAX)


def _da_call(qkv, prev_out, cst, bt, btm, lam_p, g_col, *, seq, meta_queries, lambda_init, eager_probs):
    b, tp, _ = qkv.shape
    nkb = seq // ATT_BLK
    bq = META_BLK if meta_queries else ATT_BLK
    nq = 1 if meta_queries else nkb
    q_row0 = seq // bq if meta_queries else 0
    n_t, n_tm = bt.shape[1], btm.shape[1]
    in_specs = [
        pl.BlockSpec(memory_space=pltpu.SMEM),
        pl.BlockSpec((1, bq, LANES), lambda i, h, j: (i, q_row0 + j, h)),
        pl.BlockSpec((1, tp, LANES), lambda i, h, j: (i, 0, DA_HEADS + h)),
        pl.BlockSpec((1, tp, LANES), lambda i, h, j: (i, 0, 2 * DA_HEADS + h)),
        pl.BlockSpec((1, n_t, ATT_BLK, bq), lambda i, h, j: (h, 0, 0, 0)),
        pl.BlockSpec((1, n_tm, META_BLK, bq), lambda i, h, j: (h, 0, 0, 0)),
        pl.BlockSpec((4, DA_HEAD_DIM), lambda i, h, j: (0, 0)),
        pl.BlockSpec((LANES, 1), lambda i, h, j: (0, 0)),
    ]
    args = [cst, qkv, qkv, qkv, bt, btm, lam_p, g_col]
    aliases = {}
    if prev_out is not None:
        in_specs.append(pl.BlockSpec(memory_space=pl.ANY))
        args.append(prev_out)
        aliases = {len(args) - 1: 0}
    kern = functools.partial(_da_kernel, bq=bq, nkb=nkb, n_near=(n_t + 1) // 2, seq=seq,
                             meta_queries=meta_queries, lambda_init=lambda_init, eager_probs=eager_probs)
    if prev_out is not None:
        body = kern
        kern = lambda *refs: body(*refs[:8], *refs[9:])
    return pl.pallas_call(
        kern,
        out_shape=jax.ShapeDtypeStruct((b, tp, D_MODEL), BF16),
        grid=(b, DA_HEADS, nq),
        in_specs=in_specs,
        out_specs=pl.BlockSpec((1, bq, LANES), lambda i, h, j: (i, q_row0 + j, h)),
        scratch_shapes=[pltpu.VMEM((nkb, LANES + ONES_ROWS, ATT_BLK), BF16),
                        pltpu.VMEM((LANES + ONES_ROWS, META_BLK), BF16), pltpu.VMEM((2 * bq, LANES), BF16)]
        + [pltpu.VMEM(shape, dt)
           for shape, dt in ([((1, COL_CHUNK), F32), ((1, COL_CHUNK), F32), ((LANES + ONES_ROWS, COL_CHUNK), F32)]
                             + [((ATT_BLK, COL_CHUNK), BF16 if eager_probs else F32)] * LOGIT_BUFS
                             + [((1, COL_CHUNK), F32)] * LOGIT_BUFS)
           for _ in range(2 * bq // COL_CHUNK)],
        input_output_aliases=aliases,
        compiler_params=_params(("parallel", "parallel", "arbitrary")),
        name="da_meta_queries" if meta_queries else "da_attention",
    )(*args)


def kernel(x, meta_tokens, norm_g, ffn_w_gate, ffn_w_up, ffn_w_down, na_w_qkv, na_b_qkv, na_w_o, na_b_o, na_rpb, na_meta_bias, da_w_qkv, da_w_o, da_lambda, da_subln_g, t5_rel_bias):
    b, seq, d = x.shape
    depth = norm_g.shape[0]
    assert d == D_MODEL and seq % ATT_BLK == 0 and seq % GRID_W == 0
    tp = seq + META_BLK
    meta = jnp.broadcast_to(meta_tokens[None].astype(x.dtype), (b, N_META, d))
    pad = jnp.zeros((b, META_BLK - N_META, d), x.dtype)
    h = jnp.concatenate([x, meta, pad], axis=1).reshape(b * tp, d)

    vec = lambda v: v.reshape(1, -1).astype(F32)
    zero_qkv_bias = jnp.zeros((1, 3 * d), F32)
    zero_o_bias = jnp.zeros((1, d), F32)
    da_cst = _da_tables(t5_rel_bias, seq)

    for i in range(depth):
        g = norm_g[i]
        j = i // N_MIXERS
        h = _ffn(h, vec(g[0]), vec(g[1]), ffn_w_gate[i, 0].astype(BF16), ffn_w_up[i, 0].astype(BF16),
                 ffn_w_down[i, 0].astype(BF16))
        if i % N_MIXERS == 0:
            qkv = _qkv(h, vec(g[2]), na_w_qkv[j].astype(BF16), vec(na_b_qkv[j]), (D_MODEL // NA_HEADS) ** -0.5)
            nab, mb = _na_tables(na_rpb[j], na_meta_bias[j], seq // GRID_W)
            a = _na_attention(qkv.reshape(b, tp, 3 * d), nab, mb, seq)
            w_o, b_o = na_w_o[j], vec(na_b_o[j])
        else:
            lambda_init = 0.8 - 0.6 * math.exp(-0.3 * i)
            qkv, qkv_stats = _qkv(h, vec(g[2]), da_w_qkv[j].astype(BF16), zero_qkv_bias,
                                  DA_HEAD_DIM ** -0.5 * LOG2E, with_stats=True)
            qkv = qkv.reshape(b, tp, 3 * d)
            cst, bt, btm, bt_q, btm_q = da_cst
            g_col = da_subln_g[j].reshape(LANES, 1).astype(F32)
            lam_p = da_lambda[j].astype(F32)

            def attend(eager_probs):
                a = _da_call(qkv, None, cst, bt, btm, lam_p, g_col, seq=seq, meta_queries=False,
                             lambda_init=lambda_init, eager_probs=eager_probs)
                return _da_call(qkv, a, cst, bt_q, btm_q, lam_p, g_col, seq=seq, meta_queries=True,
                                lambda_init=lambda_init, eager_probs=eager_probs)

            a = lax.cond(_da_logits_bounded(qkv_stats, t5_rel_bias), functools.partial(attend, True),
                         functools.partial(attend, False))
            w_o, b_o = da_w_o[j], zero_o_bias
        h = _mix_ffn(a, h.reshape(b, tp, d), w_o.astype(BF16), b_o, vec(g[3]), vec(g[4]), vec(g[5]),
                     ffn_w_gate[i, 1].astype(BF16), ffn_w_up[i, 1].astype(BF16), ffn_w_down[i, 1].astype(BF16),
                     out_rows=seq if i == depth - 1 else None)
        if i < depth - 1:
            h = h.reshape(b * tp, d)
    return h
```

```python
import functools
import math

import numpy as np
import jax
import jax.numpy as jnp
from jax import lax
from jax.experimental import pallas as pl
from jax.experimental.pallas import tpu as pltpu

F32 = jnp.float32
BF16 = jnp.bfloat16

D_MODEL = 1024
N_META = 16
GRID_W = 64
NA_KH = 8
NA_KW = 16
NA_HEADS = 16
DA_HEADS = 8
DA_HEAD_DIM = 64
T5_BUCKETS = 32
T5_MAX_DIST = 128
D_FF = 2816
FFN_RES = 0.5
RMS_EPS = 1e-6
N_MIXERS = 2

LANES = 128
META_BLK = 128
ROW_TILES = (1280, 640, 512, 128)
MIX_ROW_TILES = (1024, 640, 512, 128)
FFN_HIDDEN_CHUNK = 256
ATT_BLK = 512
COL_CHUNK = 256
ROW_PIECE = 64
NA_ROWS_PER_STEP = 8
MXU_DEPTH = 256
LOGIT_BUFS = 3
ONES_ROWS = 16
EAGER_HALF_RANGE = 45.0
NORM_SLACK = 1.02
EAGER_V_MAX = 2.0 ** 20
LOG2E = math.log2(math.e)
NEG = -1e30
VMEM_LIMIT = 56 * 1024 * 1024


def _rms(x, g):
    return x * lax.rsqrt(jnp.mean(x * x, axis=-1, keepdims=True) + RMS_EPS) * g


def _params(sem):
    return pltpu.CompilerParams(dimension_semantics=sem, vmem_limit_bytes=VMEM_LIMIT)


def _row_tile(rows, candidates):
    return next(t for t in candidates if rows % t == 0)


def _resident(shape):
    return pl.BlockSpec(shape, lambda *_: (0,) * len(shape), pipeline_mode=pl.Buffered(1))


def _swiglu_step(x, gin_ref, gout_ref, wg_ref, wu_ref, wd_ref):
    xn = _rms(x, gin_ref[...]).astype(BF16)
    acc = None
    for c in range(D_FF // FFN_HIDDEN_CHUNK):
        sl = slice(c * FFN_HIDDEN_CHUNK, (c + 1) * FFN_HIDDEN_CHUNK)
        g = jnp.dot(xn, wg_ref[:, sl], preferred_element_type=F32)
        u = jnp.dot(xn, wu_ref[:, sl], preferred_element_type=F32)
        hid = (g * jax.nn.sigmoid(g) * u).astype(BF16)
        d = jnp.dot(hid, wd_ref[sl, :], preferred_element_type=F32)
        acc = d if acc is None else acc + d
    return x + FFN_RES * _rms(acc, gout_ref[...])


def _ffn_kernel(h_ref, gin_ref, gout_ref, wg_ref, wu_ref, wd_ref, o_ref):
    o_ref[...] = _swiglu_step(h_ref[...], gin_ref, gout_ref, wg_ref, wu_ref, wd_ref)


def _ffn_weight_specs():
    return [_resident((1, D_MODEL)), _resident((1, D_MODEL)),
            _resident((D_MODEL, D_FF)), _resident((D_MODEL, D_FF)), _resident((D_FF, D_MODEL))]


def _ffn(h, g_in, g_out, wg, wu, wd):
    mt = h.shape[0]
    tile = _row_tile(mt, ROW_TILES)
    row = pl.BlockSpec((tile, D_MODEL), lambda i: (i, 0))
    return pl.pallas_call(
        _ffn_kernel,
        out_shape=jax.ShapeDtypeStruct((mt, D_MODEL), F32),
        grid=(mt // tile,),
        in_specs=[row] + _ffn_weight_specs(),
        out_specs=row,
        compiler_params=_params(("parallel",)),
        name="ffn",
    )(h, g_in, g_out, wg, wu, wd)


def _mix_ffn_kernel(a_ref, h_ref, wo_ref, bo_ref, gm_ref, gin_ref, gout_ref, wg_ref, wu_ref, wd_ref, o_ref):
    m = jnp.dot(a_ref[...], wo_ref[...], preferred_element_type=F32) + bo_ref[...]
    h1 = h_ref[...] + _rms(m, gm_ref[...])
    o_ref[...] = _swiglu_step(h1, gin_ref, gout_ref, wg_ref, wu_ref, wd_ref)


def _mix_ffn(a, h, wo, bo, g_mix, g_in, g_out, wg, wu, wd, out_rows=None):
    b, tp, _ = h.shape
    rows = tp if out_rows is None else out_rows
    tile = _row_tile(rows, MIX_ROW_TILES)
    blk = lambda width: pl.BlockSpec((None, tile, width), lambda i, j: (i, j, 0))
    return pl.pallas_call(
        _mix_ffn_kernel,
        out_shape=jax.ShapeDtypeStruct((b, rows, D_MODEL), F32),
        grid=(b, rows // tile),
        in_specs=[blk(D_MODEL), blk(D_MODEL), _resident((D_MODEL, D_MODEL)), _resident((1, D_MODEL)),
                  _resident((1, D_MODEL))] + _ffn_weight_specs(),
        out_specs=blk(D_MODEL),
        compiler_params=_params(("parallel", "parallel")),
        name="mix_ffn",
    )(a, h, wo, bo, g_mix, g_in, g_out, wg, wu, wd)


def _qkv_kernel(h_ref, g_ref, w_ref, b_ref, *rest, q_scale, with_stats):
    if with_stats:
        ind_ref, o_ref, st_ref = rest
    else:
        (o_ref,) = rest
    xn = _rms(h_ref[...], g_ref[...]).astype(BF16)
    stats = []
    for j in range(3):
        sl = slice(j * D_MODEL, (j + 1) * D_MODEL)
        y = jnp.dot(xn, w_ref[:, sl], preferred_element_type=F32) + b_ref[:, sl]
        if j == 0:
            y = y * q_scale
        o_ref[:, sl] = y.astype(BF16)
        if with_stats and j < 2:
            norm2 = jnp.dot((y * y).astype(BF16), ind_ref[...], preferred_element_type=F32)
            stats.append(jnp.max(norm2, axis=0, keepdims=True))
        elif with_stats:
            a = jnp.abs(y)
            fold = a[:, 0:LANES]
            for t in range(1, D_MODEL // LANES):
                fold = jnp.maximum(fold, a[:, t * LANES:(t + 1) * LANES])
            stats.append(jnp.max(fold, axis=0, keepdims=True))
    if with_stats:
        st_ref[0] = jnp.concatenate(stats + [jnp.zeros((8 - len(stats), LANES), F32)], axis=0)


def _qkv(h, g, w, b, q_scale, with_stats=False):
    mt = h.shape[0]
    tile = _row_tile(mt, ROW_TILES)
    n_tile = mt // tile
    in_specs = [pl.BlockSpec((tile, D_MODEL), lambda i: (i, 0)), _resident((1, D_MODEL)),
                _resident((D_MODEL, 3 * D_MODEL)), _resident((1, 3 * D_MODEL))]
    out_shape = [jax.ShapeDtypeStruct((mt, 3 * D_MODEL), BF16)]
    out_specs = [pl.BlockSpec((tile, 3 * D_MODEL), lambda i: (i, 0))]
    args = [h, g, w, b]
    if with_stats:
        ind = (np.arange(D_MODEL)[:, None] // DA_HEAD_DIM == np.arange(LANES)[None, :])
        args.append(jnp.asarray(ind, BF16))
        in_specs.append(_resident((D_MODEL, LANES)))
        out_shape.append(jax.ShapeDtypeStruct((n_tile, 8, LANES), F32))
        out_specs.append(pl.BlockSpec((1, 8, LANES), lambda i: (i, 0, 0)))
    out = pl.pallas_call(
        functools.partial(_qkv_kernel, q_scale=q_scale, with_stats=with_stats),
        out_shape=out_shape,
        grid=(n_tile,),
        in_specs=in_specs,
        out_specs=out_specs,
        compiler_params=_params(("parallel",)),
        name="qkv_proj",
    )(*args)
    return out if with_stats else out[0]


def _na_kernel(q_ref, k_ref, v_ref, nab_ref, mb_ref, o_ref, *, rows, seq):
    kh = NA_KH
    lane = lax.broadcasted_iota(jnp.int32, (GRID_W, LANES), 1)
    head0 = lane < (LANES // 2)
    k_meta = k_ref[0, seq:seq + META_BLK, :]
    v_meta = v_ref[0, seq:seq + META_BLK, :]
    mb = jnp.concatenate([jnp.broadcast_to(mb_ref[0, 0], (GRID_W, LANES)),
                          jnp.broadcast_to(mb_ref[0, 1], (GRID_W, LANES))], axis=0)
    nt = (((1,), (1,)), ((), ()))

    def split_heads(q):
        zero = jnp.zeros_like(q)
        return jnp.concatenate([jnp.where(head0[:q.shape[0]], q, zero),
                                jnp.where(head0[:q.shape[0]], zero, q)], axis=0)

    def row_group(i, carry):
        row_ids = [i * NA_ROWS_PER_STEP + u for u in range(NA_ROWS_PER_STEP)]
        q2s = [split_heads(q_ref[0, pl.ds(pl.multiple_of(r * GRID_W, GRID_W), GRID_W), :]) for r in row_ids]
        sm_all = lax.dot_general(jnp.concatenate(q2s, axis=0), k_meta, nt, preferred_element_type=F32)
        partial = []
        for u, r in enumerate(row_ids):
            rs = jnp.clip(r - kh // 2, 0, rows - kh)
            dy0 = rs - r + (NA_KH - 1)
            kbase = pl.multiple_of(rs * GRID_W, GRID_W)
            k_win = k_ref[0, pl.ds(kbase, kh * GRID_W), :]
            v_win = v_ref[0, pl.ds(kbase, kh * GRID_W), :]
            bias = jnp.concatenate([nab_ref[0, 0, dy0], nab_ref[0, 1, dy0]], axis=0)
            s = lax.dot_general(q2s[u], k_win, nt, preferred_element_type=F32) + bias
            sm = sm_all[u * LANES:(u + 1) * LANES] + mb
            m = jnp.maximum(jnp.max(s, axis=-1, keepdims=True), jnp.max(sm, axis=-1, keepdims=True))
            p = jnp.exp(s - m)
            pm = jnp.exp(sm - m)
            l = jnp.sum(p, axis=-1, keepdims=True) + jnp.sum(pm, axis=-1, keepdims=True)
            partial.append((jnp.dot(p.astype(BF16), v_win, preferred_element_type=F32), l, pm.astype(BF16)))
        om_all = jnp.dot(jnp.concatenate([t[2] for t in partial], axis=0), v_meta, preferred_element_type=F32)
        for u, r in enumerate(row_ids):
            o2 = (partial[u][0] + om_all[u * LANES:(u + 1) * LANES]) / partial[u][1]
            o = jnp.where(head0, o2[:GRID_W], o2[GRID_W:])
            o_ref[0, pl.ds(pl.multiple_of(r * GRID_W, GRID_W), GRID_W), :] = o.astype(BF16)
        return carry

    lax.fori_loop(0, rows // NA_ROWS_PER_STEP, row_group, 0)

    qm = q_ref[0, seq:seq + META_BLK, :]
    lane_m = lax.broadcasted_iota(jnp.int32, (META_BLK, LANES), 1) < (LANES // 2)
    zero = jnp.zeros_like(qm)
    q2 = jnp.concatenate([jnp.where(lane_m, qm, zero), jnp.where(lane_m, zero, qm)], axis=0)
    mbm = jnp.concatenate([jnp.broadcast_to(mb_ref[0, 0], (META_BLK, LANES)),
                           jnp.broadcast_to(mb_ref[0, 1], (META_BLK, LANES))], axis=0)
    sm = lax.dot_general(q2, k_meta, nt, preferred_element_type=F32) + mbm
    pm = jnp.exp(sm - jnp.max(sm, axis=-1, keepdims=True))
    o2 = jnp.dot(pm.astype(BF16), v_meta, preferred_element_type=F32) / jnp.sum(pm, axis=-1, keepdims=True)
    o_ref[0, seq:seq + META_BLK, :] = jnp.where(lane_m, o2[:META_BLK], o2[META_BLK:]).astype(BF16)


def _toeplitz(w, n, m):
    period = w.shape[-1]
    width = period - 1
    assert m <= width
    reps = -(-(n * width) // period)
    flat = jnp.tile(w, (1,) * (w.ndim - 1) + (reps,))[..., :n * width]
    return flat.reshape(w.shape[:-1] + (n, width))[..., :m]


def _na_tables(rpb, meta_bias, rows):
    kh = NA_KH
    qc = np.arange(GRID_W)[:, None]
    kc = np.arange(GRID_W)[None, :]
    cs = np.clip(qc - NA_KW // 2, 0, GRID_W - NA_KW)
    valid = (kc >= cs) & (kc < cs + NA_KW)
    period = 2 * GRID_W
    off = np.arange(period)
    off = np.where(off < GRID_W, off, off - period)
    w = rpb[:, :, np.clip(off + NA_KW - 1, 0, 2 * NA_KW - 2)]
    t = jnp.where(valid, _toeplitz(w, GRID_W, GRID_W), NEG)
    t = jnp.stack([t[:, dy0:dy0 + kh].transpose(0, 2, 1, 3) for dy0 in range(kh)], axis=1)
    t = t.reshape(NA_HEADS // 2, 2, kh, GRID_W, kh * GRID_W)
    mb = jnp.full((NA_HEADS, LANES), NEG, F32).at[:, :N_META].set(meta_bias)
    return t.astype(F32), mb.reshape(NA_HEADS // 2, 2, 1, LANES)


def _na_attention(qkv, nab, mb, seq):
    b, tp, _ = qkv.shape
    rows = seq // GRID_W
    n_slab = D_MODEL // LANES
    slab = lambda off: pl.BlockSpec((1, tp, LANES), lambda i, s: (i, 0, off + s))
    return pl.pallas_call(
        functools.partial(_na_kernel, rows=rows, seq=seq),
        out_shape=jax.ShapeDtypeStruct((b, tp, D_MODEL), BF16),
        grid=(b, n_slab),
        in_specs=[slab(0), slab(n_slab), slab(2 * n_slab),
                  pl.BlockSpec((1, 2, NA_KH, GRID_W, NA_KH * GRID_W), lambda i, s: (s, 0, 0, 0, 0)),
                  pl.BlockSpec((1, 2, 1, LANES), lambda i, s: (s, 0, 0, 0))],
        out_specs=slab(0),
        compiler_params=_params(("parallel", "parallel")),
        name="na_attention",
    )(qkv, qkv, qkv, nab, mb)


def _da_kernel(cst_ref, q_ref, k_ref, v_ref, bt_ref, btm_ref, lamp_ref, g_ref, o_ref,
               vt_scr, vtm_scr, q2_scr, *chunk_scr, bq, nkb, n_near, seq, meta_queries, lambda_init,
               eager_probs):
    bk = ATT_BLK
    n_chunk = 2 * bq // COL_CHUNK
    groups = [chunk_scr[i * n_chunk:(i + 1) * n_chunk] for i in range(3 + 2 * LOGIT_BUFS)]
    m_scr, mlast_scr, acc_scr = groups[:3]
    blk_scr, row_scr = groups[3:3 + LOGIT_BUFS], groups[3 + LOGIT_BUFS:]
    hd = pl.program_id(1)
    qi = pl.program_id(2)
    nt = (((1,), (1,)), ((), ()))

    @pl.when(qi == 0)
    def _():
        for kb in range(nkb):
            vt_scr[kb, 0:LANES, :] = v_ref[0, kb * bk:(kb + 1) * bk, :].astype(F32).T.astype(BF16)
            vt_scr[kb, LANES:, :] = jnp.ones((ONES_ROWS, bk), BF16)
        vtm_scr[0:LANES, :] = v_ref[0, seq:seq + META_BLK, :].astype(F32).T.astype(BF16)
        vtm_scr[LANES:, :] = jnp.ones((ONES_ROWS, META_BLK), BF16)

    q = q_ref[0]
    first = lax.broadcasted_iota(jnp.int32, q.shape, 1) < DA_HEAD_DIM
    zero = jnp.zeros_like(q)
    q2_scr[0:bq, :] = jnp.where(first, q, zero)
    q2_scr[bq:2 * bq, :] = jnp.where(first, zero, q)
    for cc in range(n_chunk):
        m_scr[cc][...] = jnp.full(m_scr[cc].shape, NEG, F32)
        mlast_scr[cc][...] = jnp.full(mlast_scr[cc].shape, NEG, F32)
        acc_scr[cc][...] = jnp.zeros(acc_scr[cc].shape, F32)
    c_neg = cst_ref[hd, 0]
    c_pos = cst_ref[hd, 1]

    def logits(k_blk, buf, c, bias=None, first=False):
        nk = k_blk.shape[0]
        for cc in range(n_chunk):
            s = lax.dot_general(k_blk, q2_scr[cc * COL_CHUNK:(cc + 1) * COL_CHUNK, :], nt,
                                preferred_element_type=F32)
            if bias is not None:
                if bq >= COL_CHUNK:
                    b0 = (cc * COL_CHUNK) % bq
                    s = s + bias[:, b0:b0 + COL_CHUNK]
                else:
                    s = s + jnp.concatenate([bias[...]] * (COL_CHUNK // bq), axis=1)
            s_max = jnp.max(s, axis=0, keepdims=True) + c
            if not eager_probs:
                blk_scr[buf][cc][0:nk, :] = s
                row_scr[buf][cc][...] = s_max
                continue
            m_run = m_scr[cc][...]
            scale = s_max if first else m_run
            blk_scr[buf][cc][0:nk, :] = jnp.exp2(s - (scale - c)).astype(BF16)
            row_scr[buf][cc][...] = jnp.exp2(mlast_scr[cc][...] - scale)
            mlast_scr[cc][...] = scale
            m_scr[cc][...] = s_max if first else jnp.maximum(m_run, s_max)

    def softmax_pv(vt, c, buf):
        nk = vt.shape[1]
        kd = min(nk, MXU_DEPTH)
        for cc in range(n_chunk):
            if eager_probs:
                acc = acc_scr[cc][...] * row_scr[buf][cc][...]
                for k0 in range(0, nk, kd):
                    acc = acc + jnp.dot(vt[:, k0:k0 + kd], blk_scr[buf][cc][k0:k0 + kd, :],
                                        preferred_element_type=F32)
            else:
                m_old = m_scr[cc][...]
                m_new = jnp.maximum(m_old, row_scr[buf][cc][...])
                shift = m_new - c
                acc = acc_scr[cc][...] * jnp.exp2(m_old - m_new)
                for k0 in range(0, nk, kd):
                    p = jnp.concatenate(
                        [jnp.exp2(blk_scr[buf][cc][r0:r0 + ROW_PIECE, :] - shift).astype(BF16)
                         for r0 in range(k0, k0 + kd, ROW_PIECE)], axis=0)
                    acc = acc + jnp.dot(vt[:, k0:k0 + kd], p, preferred_element_type=F32)
                m_scr[cc][...] = m_new
            acc_scr[cc][...] = acc

    n_far = nkb - n_near
    lo = 0 if meta_queries else jnp.clip(qi - n_near // 2, 0, nkb - n_near)
    tm = 0 if meta_queries else jnp.minimum(qi, 1)
    key_blk = lambda kb: k_ref[0, pl.ds(pl.multiple_of(kb * bk, bk), bk), :]
    far_kb = lambda j: jnp.where(j < lo, j, j + n_near)

    def issue_logits(t, buf, first=False):
        if isinstance(t, int) and t == nkb:
            logits(k_ref[0, seq:seq + META_BLK, :], buf, 0.0, btm_ref.at[0, tm])
        elif isinstance(t, int) and t >= n_far:
            kb = lo + (t - n_far)
            logits(key_blk(kb), buf, 0.0, bt_ref.at[0, 0 if meta_queries else kb - qi + n_near - 1])
        else:
            logits(key_blk(far_kb(t)), buf, jnp.where(t < lo, c_neg, c_pos), first=first)

    def finish(t, buf):
        if isinstance(t, int) and t == nkb:
            softmax_pv(vtm_scr[...], 0.0, buf)
        elif isinstance(t, int) and t >= n_far:
            softmax_pv(vt_scr[lo + (t - n_far)], 0.0, buf)
        else:
            softmax_pv(vt_scr[far_kb(t)], jnp.where(t < lo, c_neg, c_pos), buf)

    issue_logits(0, 0, first=True)
    n_group = (n_far - 1) // LOGIT_BUFS

    def group_body(i, carry):
        for u in range(LOGIT_BUFS):
            issue_logits(LOGIT_BUFS * i + u + 1, (u + 1) % LOGIT_BUFS)
            finish(LOGIT_BUFS * i + u, u)
        return carry

    lax.fori_loop(0, n_group, group_body, 0)
    for t in range(LOGIT_BUFS * n_group, nkb + 1):
        if t < nkb:
            issue_logits(t + 1, (t + 1) % LOGIT_BUFS)
        finish(t, t % LOGIT_BUFS)

    lp = lamp_ref[...]
    lam = (jnp.exp(jnp.sum(lp[0:1] * lp[1:2], axis=1, keepdims=True))
           - jnp.exp(jnp.sum(lp[2:3] * lp[3:4], axis=1, keepdims=True)) + lambda_init)
    on = jnp.concatenate([acc_scr[cc][0:LANES, :] / acc_scr[cc][LANES:LANES + 1, :] for cc in range(n_chunk)],
                         axis=1)
    ot = on[:, :bq] - lam * on[:, bq:]
    y = ot * lax.rsqrt(jnp.mean(ot * ot, axis=0, keepdims=True) + RMS_EPS) * g_ref[...]
    o_ref[0] = (y * (1.0 - lambda_init)).T.astype(BF16)


def _t5_bucket(rel):
    nb = T5_BUCKETS // 2
    max_exact = nb // 2
    ret = jnp.where(rel > 0, nb, 0)
    n = jnp.abs(rel)
    nf = jnp.maximum(n, 1).astype(jnp.float32)
    large = max_exact + (jnp.log(nf / max_exact) / math.log(T5_MAX_DIST / max_exact)
                         * (nb - max_exact)).astype(jnp.int32)
    large = jnp.minimum(large, nb - 1)
    return ret + jnp.where(n < max_exact, n, large)


def _da_tables(rel_table, seq):
    blk = ATT_BLK
    n_head = rel_table.shape[1]
    reach = 2 * blk - 1
    rel_table = rel_table.astype(F32) * LOG2E
    line = rel_table[_t5_bucket(jnp.arange(-reach, reach + 1, dtype=jnp.int32))].T
    val = lambda rel: line[:, np.clip(np.asarray(rel) + reach, 0, 2 * reach)]
    cst = jnp.stack([rel_table[T5_BUCKETS // 2 - 1], rel_table[T5_BUCKETS - 1]], axis=1)
    qq = np.arange(blk)[None, :]
    period = 2 * blk
    off = np.arange(period)
    off = np.where(off < blk, off, off - period)
    flat = lambda col: jnp.broadcast_to(cst[:, col, None, None], (n_head, blk, blk))
    bt = jnp.stack([flat(0)] + [_toeplitz(val(d * blk - off), blk, blk) for d in (-1, 0, 1)] + [flat(1)],
                   axis=1)
    jm = np.arange(N_META)[:, None]
    first = val(jm - (qq + N_META))
    later = jnp.broadcast_to(cst[:, 0, None, None], first.shape)
    pad_keys = lambda cols, n: jnp.full((n_head, n, META_BLK - N_META, cols), NEG, F32)
    btm = jnp.concatenate([jnp.stack([first, later], axis=1), pad_keys(blk, 2)], axis=2)
    pad_cols = lambda t: jnp.concatenate(
        [t, jnp.broadcast_to(t[..., -1:], t.shape[:-1] + (META_BLK - N_META,))], axis=-1)
    im = np.arange(N_META)[None, :]
    bt_q = pad_cols(val(np.arange(blk)[:, None] + N_META - im))[:, None]
    btm_q = jnp.concatenate([pad_cols(val(jm - im))[:, None], pad_keys(META_BLK, 1)], axis=2)
    return cst, bt, btm, bt_q, btm_q


def _da_logits_bounded(stats, rel_table):
    st = jnp.max(stats, axis=0)
    n_half = 2 * DA_HEADS
    dot_bound = jnp.sqrt(st[0, :n_half] * st[1, :n_half]).reshape(DA_HEADS, 2) * NORM_SLACK
    bias_bound = jnp.max(jnp.abs(rel_table.astype(F32)), axis=0) * LOG2E
    logit_bound = jnp.max(jnp.max(dot_bound, axis=1) + bias_bound)
    return (logit_bound <= EAGER_HALF_RANGE) & (jnp.max(st[2]) <= EAGER_V_MAX)


def _da_call(qkv, prev_out, cst, bt, btm, lam_p, g_col, *, seq, meta_queries, lambda_init, eager_probs):
    b, tp, _ = qkv.shape
    nkb = seq // ATT_BLK
    bq = META_BLK if meta_queries else ATT_BLK
    nq = 1 if meta_queries else nkb
    q_row0 = seq // bq if meta_queries else 0
    n_t, n_tm = bt.shape[1], btm.shape[1]
    in_specs = [
        pl.BlockSpec(memory_space=pltpu.SMEM),
        pl.BlockSpec((1, bq, LANES), lambda i, h, j: (i, q_row0 + j, h)),
        pl.BlockSpec((1, tp, LANES), lambda i, h, j: (i, 0, DA_HEADS + h)),
        pl.BlockSpec((1, tp, LANES), lambda i, h, j: (i, 0, 2 * DA_HEADS + h)),
        pl.BlockSpec((1, n_t, ATT_BLK, bq), lambda i, h, j: (h, 0, 0, 0)),
        pl.BlockSpec((1, n_tm, META_BLK, bq), lambda i, h, j: (h, 0, 0, 0)),
        pl.BlockSpec((4, DA_HEAD_DIM), lambda i, h, j: (0, 0)),
        pl.BlockSpec((LANES, 1), lambda i, h, j: (0, 0)),
    ]
    args = [cst, qkv, qkv, qkv, bt, btm, lam_p, g_col]
    aliases = {}
    if prev_out is not None:
        in_specs.append(pl.BlockSpec(memory_space=pl.ANY))
        args.append(prev_out)
        aliases = {len(args) - 1: 0}
    kern = functools.partial(_da_kernel, bq=bq, nkb=nkb, n_near=(n_t + 1) // 2, seq=seq,
                             meta_queries=meta_queries, lambda_init=lambda_init, eager_probs=eager_probs)
    if prev_out is not None:
        body = kern
        kern = lambda *refs: body(*refs[:8], *refs[9:])
    return pl.pallas_call(
        kern,
        out_shape=jax.ShapeDtypeStruct((b, tp, D_MODEL), BF16),
        grid=(b, DA_HEADS, nq),
        in_specs=in_specs,
        out_specs=pl.BlockSpec((1, bq, LANES), lambda i, h, j: (i, q_row0 + j, h)),
        scratch_shapes=[pltpu.VMEM((nkb, LANES + ONES_ROWS, ATT_BLK), BF16),
                        pltpu.VMEM((LANES + ONES_ROWS, META_BLK), BF16), pltpu.VMEM((2 * bq, LANES), BF16)]
        + [pltpu.VMEM(shape, dt)
           for shape, dt in ([((1, COL_CHUNK), F32), ((1, COL_CHUNK), F32), ((LANES + ONES_ROWS, COL_CHUNK), F32)]
                             + [((ATT_BLK, COL_CHUNK), BF16 if eager_probs else F32)] * LOGIT_BUFS
                             + [((1, COL_CHUNK), F32)] * LOGIT_BUFS)
           for _ in range(2 * bq // COL_CHUNK)],
        input_output_aliases=aliases,
        compiler_params=_params(("parallel", "parallel", "arbitrary")),
        name="da_meta_queries" if meta_queries else "da_attention",
    )(*args)


def kernel(x, meta_tokens, norm_g, ffn_w_gate, ffn_w_up, ffn_w_down, na_w_qkv, na_b_qkv, na_w_o, na_b_o, na_rpb, na_meta_bias, da_w_qkv, da_w_o, da_lambda, da_subln_g, t5_rel_bias):
    b, seq, d = x.shape
    depth = norm_g.shape[0]
    assert d == D_MODEL and seq % ATT_BLK == 0 and seq % GRID_W == 0
    tp = seq + META_BLK
    meta = jnp.broadcast_to(meta_tokens[None].astype(x.dtype), (b, N_META, d))
    pad = jnp.zeros((b, META_BLK - N_META, d), x.dtype)
    h = jnp.concatenate([x, meta, pad], axis=1).reshape(b * tp, d)

    vec = lambda v: v.reshape(1, -1).astype(F32)
    zero_qkv_bias = jnp.zeros((1, 3 * d), F32)
    zero_o_bias = jnp.zeros((1, d), F32)
    da_cst = _da_tables(t5_rel_bias, seq)

    for i in range(depth):
        g = norm_g[i]
        j = i // N_MIXERS
        h = _ffn(h, vec(g[0]), vec(g[1]), ffn_w_gate[i, 0].astype(BF16), ffn_w_up[i, 0].astype(BF16),
                 ffn_w_down[i, 0].astype(BF16))
        if i % N_MIXERS == 0:
            qkv = _qkv(h, vec(g[2]), na_w_qkv[j].astype(BF16), vec(na_b_qkv[j]), (D_MODEL // NA_HEADS) ** -0.5)
            nab, mb = _na_tables(na_rpb[j], na_meta_bias[j], seq // GRID_W)
            a = _na_attention(qkv.reshape(b, tp, 3 * d), nab, mb, seq)
            w_o, b_o = na_w_o[j], vec(na_b_o[j])
        else:
            lambda_init = 0.8 - 0.6 * math.exp(-0.3 * i)
            qkv, qkv_stats = _qkv(h, vec(g[2]), da_w_qkv[j].astype(BF16), zero_qkv_bias,
                                  DA_HEAD_DIM ** -0.5 * LOG2E, with_stats=True)
            qkv = qkv.reshape(b, tp, 3 * d)
            cst, bt, btm, bt_q, btm_q = da_cst
            g_col = da_subln_g[j].reshape(LANES, 1).astype(F32)
            lam_p = da_lambda[j].astype(F32)

            def attend(eager_probs):
                a = _da_call(qkv, None, cst, bt, btm, lam_p, g_col, seq=seq, meta_queries=False,
                             lambda_init=lambda_init, eager_probs=eager_probs)
                return _da_call(qkv, a, cst, bt_q, btm_q, lam_p, g_col, seq=seq, meta_queries=True,
                                lambda_init=lambda_init, eager_probs=eager_probs)

            a = lax.cond(_da_logits_bounded(qkv_stats, t5_rel_bias), functools.partial(attend, True),
                         functools.partial(attend, False))
            w_o, b_o = da_w_o[j], zero_o_bias
        h = _mix_ffn(a, h.reshape(b, tp, d), w_o.astype(BF16), b_o, vec(g[3]), vec(g[4]), vec(g[5]),
                     ffn_w_gate[i, 1].astype(BF16), ffn_w_up[i, 1].astype(BF16), ffn_w_down[i, 1].astype(BF16),
                     out_rows=seq if i == depth - 1 else None)
        if i < depth - 1:
            h = h.reshape(b * tp, d)
    return h
```
